```python
import math
import jax, jax.numpy as jnp
from jax import lax
import numpy as np

D_MODEL = 1024
BATCH = 4
SEQ = 8192
DEPTH = 1

D_MIX = D_MODEL
D_POOL = D_MIX // 2
POOL_WINDOWS = (2, 4, 8, 16)
N_POOL_GROUPS = len(POOL_WINDOWS)
POOL_GROUP = D_POOL // N_POOL_GROUPS
D_ATTN = D_MIX - D_POOL
HEAD_DIM = 64
N_ATTN_HEADS = D_ATTN // (2 * HEAD_DIM)
D_IN = D_POOL + 3 * D_ATTN
ROPE_THETA = 10000.0
Q_BLOCK = 128
LAMBDA_STD = 0.1
D_PLE = 256
N_KEYS = 128
N_EXPERTS = N_KEYS * N_KEYS
N_PEER_HEADS = 8
PEER_TOPK = 16
D_PEER_QUERY = 256
D_SUBKEY = D_PEER_QUERY // 2
PEER_CHUNK = 128
EPS = 1e-6

kernel_name = "hybrid_pool_diffattn_peer_block"


def rmsnorm(x, gain):
    xf = x.astype(jnp.float32)
    y = xf * lax.rsqrt(jnp.mean(xf * xf, axis=-1, keepdims=True) + EPS)
    return (y * gain.astype(jnp.float32)).astype(x.dtype)


def rope(t, positions):
    d = t.shape[-1]
    inv_freq = ROPE_THETA ** (-jnp.arange(0, d, 2, dtype=jnp.float32) / d)
    ang = positions.astype(jnp.float32)[..., None] * inv_freq
    cos = jnp.cos(ang)[:, :, None, None, :]
    sin = jnp.sin(ang)[:, :, None, None, :]
    tf = t.astype(jnp.float32)
    t1, t2 = tf[..., : d // 2], tf[..., d // 2:]
    return jnp.concatenate([t1 * cos - t2 * sin, t2 * cos + t1 * sin], axis=-1).astype(t.dtype)


def pool_mixer(z, pool_w, pool_scale):
    B, S, _ = z.shape
    zg = z.astype(jnp.float32).reshape(B, S, N_POOL_GROUPS, POOL_GROUP)
    csum = lax.cumsum(zg, axis=1)
    t = jnp.arange(1, S + 1, dtype=jnp.float32)
    outs = []
    for g, w in enumerate(POOL_WINDOWS):
        cg = csum[:, :, g]
        prev = jnp.pad(cg, ((0, 0), (w, 0), (0, 0)))[:, :S]
        cnt = jnp.minimum(t, float(w))[None, :, None]
        outs.append((cg - prev) / cnt - zg[:, :, g])
    m = jnp.stack(outs, axis=2).astype(z.dtype)
    y = jnp.einsum('bsgc,gce->bsge', m, pool_w).reshape(B, S, D_POOL)
    return y * pool_scale


def diff_attention(q, k, v, positions, q_norm, k_norm, lq1, lk1, lq2, lk2, subln, layer_idx):
    B, S, H, _, d = q.shape
    q = rope(rmsnorm(q, q_norm), positions)
    k = rope(rmsnorm(k, k_norm), positions)
    lam_init = 0.8 - 0.6 * math.exp(-0.3 * layer_idx)
    lam = (jnp.exp(jnp.sum(lq1.astype(jnp.float32) * lk1.astype(jnp.float32)))
           - jnp.exp(jnp.sum(lq2.astype(jnp.float32) * lk2.astype(jnp.float32))) + lam_init)
    scale = 1.0 / math.sqrt(d)
    nb = S // Q_BLOCK
    qb = q.reshape(B, nb, Q_BLOCK, H, 2, d).swapaxes(0, 1)
    kf = k.astype(jnp.float32)
    vf = v.astype(jnp.float32)
    key_pos = jnp.arange(S)

    def block(args):
        q_blk, bi = args
        s = jnp.einsum('bqhmd,bkhmd->bhmqk', q_blk.astype(jnp.float32), kf) * scale
        qpos = bi * Q_BLOCK + jnp.arange(Q_BLOCK)
        mask = key_pos[None, :] <= qpos[:, None]
        s = jnp.where(mask, s, -jnp.inf)
        pr = jax.nn.softmax(s, axis=-1)
        a = pr[:, :, 0] - lam * pr[:, :, 1]
        return jnp.einsum('bhqk,bkhe->bqhe', a, vf)

    o = lax.map(block, (qb, jnp.arange(nb)))
    o = o.swapaxes(0, 1).reshape(B, S, H, 2 * d)
    o = rmsnorm(o, subln) * (1.0 - lam_init)
    return o.reshape(B, S, H * 2 * d).astype(v.dtype)


def peer(xn, w_q, subkeys, expert_u, expert_v):
    B, S, D = xn.shape
    q = (xn @ w_q).reshape(B, S, N_PEER_HEADS, 2, D_SUBKEY)
    sc = jnp.einsum('bshmc,hmnc->bshmn', q.astype(jnp.float32), subkeys.astype(jnp.float32))
    v_half, i_half = lax.top_k(sc, PEER_TOPK)
    cand = v_half[..., 0, :, None] + v_half[..., 1, None, :]
    cidx = i_half[..., 0, :, None] * N_KEYS + i_half[..., 1, None, :]
    cand = cand.reshape(B, S, N_PEER_HEADS, PEER_TOPK * PEER_TOPK)
    cidx = cidx.reshape(B, S, N_PEER_HEADS, PEER_TOPK * PEER_TOPK)
    top_s, pos = lax.top_k(cand, PEER_TOPK)
    eidx = jnp.take_along_axis(cidx, pos, axis=-1)
    g = jax.nn.softmax(top_s, axis=-1)
    nc = (B * S) // PEER_CHUNK
    xc = xn.reshape(nc, PEER_CHUNK, D)
    ic = eidx.reshape(nc, PEER_CHUNK, N_PEER_HEADS * PEER_TOPK)
    gc = g.reshape(nc, PEER_CHUNK, N_PEER_HEADS * PEER_TOPK).astype(xn.dtype)

    def chunk(args):
        xt, it, gt = args
        u = jnp.take(expert_u, it, axis=0)
        a = jnp.einsum('tc,tec->te', xt, u)
        hid = jax.nn.gelu(a, approximate=False) * gt
        vv = jnp.take(expert_v, it, axis=0)
        return jnp.einsum('te,tec->tc', hid, vv)

    out = lax.map(chunk, (xc, ic, gc))
    return out.reshape(B, S, D)


def setup_inputs(seed: int = 0) -> dict:
    key = jax.random.key(seed)
    ks = jax.random.split(key, 24)
    f32 = jnp.float32
    nrm = lambda k, shape, std: jax.random.normal(k, shape, f32) * std
    gain = lambda k, shape: 1.0 + 0.02 * jax.random.normal(k, shape, f32)
    return {
        "x": jax.random.normal(ks[0], (BATCH, SEQ, D_MODEL), f32),
        "p": jax.random.normal(ks[1], (DEPTH, BATCH, SEQ, D_PLE), f32),
        "positions": jnp.broadcast_to(jnp.arange(SEQ, dtype=jnp.int32), (BATCH, SEQ)),
        "ln_mix": gain(ks[2], (DEPTH, D_MODEL)),
        "w_in": nrm(ks[3], (DEPTH, D_MODEL, D_IN), D_MODEL ** -0.5),
        "pool_w": nrm(ks[4], (DEPTH, N_POOL_GROUPS, POOL_GROUP, POOL_GROUP), POOL_GROUP ** -0.5),
        "pool_scale": gain(ks[5], (DEPTH, D_POOL)),
        "q_norm": gain(ks[6], (DEPTH, HEAD_DIM)),
        "k_norm": gain(ks[7], (DEPTH, HEAD_DIM)),
        "lambda_q1": nrm(ks[8], (DEPTH, HEAD_DIM), LAMBDA_STD),
        "lambda_k1": nrm(ks[9], (DEPTH, HEAD_DIM), LAMBDA_STD),
        "lambda_q2": nrm(ks[10], (DEPTH, HEAD_DIM), LAMBDA_STD),
        "lambda_k2": nrm(ks[11], (DEPTH, HEAD_DIM), LAMBDA_STD),
        "subln": gain(ks[12], (DEPTH, 2 * HEAD_DIM)),
        "w_o": nrm(ks[13], (DEPTH, D_MIX, D_MODEL), D_MIX ** -0.5),
        "ln_ffn": gain(ks[14], (DEPTH, D_MODEL)),
        "w_peer_q": nrm(ks[15], (DEPTH, D_MODEL, N_PEER_HEADS * D_PEER_QUERY), D_MODEL ** -0.5),
        "peer_subkeys": nrm(ks[16], (DEPTH, N_PEER_HEADS, 2, N_KEYS, D_SUBKEY), D_SUBKEY ** -0.5),
        "peer_u": nrm(ks[17], (DEPTH, N_EXPERTS, D_MODEL), D_MODEL ** -0.5),
        "peer_v": nrm(ks[18], (DEPTH, N_EXPERTS, D_MODEL), N_PEER_HEADS ** -0.5),
        "ln_pe": gain(ks[19], (DEPTH, D_MODEL)),
        "w_pe_gate": nrm(ks[20], (DEPTH, D_MODEL, D_MODEL), D_MODEL ** -0.5),
        "w_pe_proj": nrm(ks[21], (DEPTH, D_PLE, D_MODEL), D_PLE ** -0.5),
    }


def reference(x, p, positions, ln_mix, w_in, pool_w, pool_scale, q_norm, k_norm,
              lambda_q1, lambda_k1, lambda_q2, lambda_k2, subln, w_o, ln_ffn,
              w_peer_q, peer_subkeys, peer_u, peer_v, ln_pe, w_pe_gate, w_pe_proj):
    B, S, _ = x.shape
    h = x
    for i in range(DEPTH):
        xn = rmsnorm(h, ln_mix[i])
        z = xn @ w_in[i]
        z_pool = z[..., :D_POOL]
        zq = z[..., D_POOL:D_POOL + D_ATTN].reshape(B, S, N_ATTN_HEADS, 2, HEAD_DIM)
        zk = z[..., D_POOL + D_ATTN:D_POOL + 2 * D_ATTN].reshape(B, S, N_ATTN_HEADS, 2, HEAD_DIM)
        zv = z[..., D_POOL + 2 * D_ATTN:].reshape(B, S, N_ATTN_HEADS, 2 * HEAD_DIM)
        y_pool = pool_mixer(z_pool, pool_w[i], pool_scale[i])
        y_attn = diff_attention(zq, zk, zv, positions, q_norm[i], k_norm[i],
                                lambda_q1[i], lambda_k1[i], lambda_q2[i], lambda_k2[i],
                                subln[i], i)
        h = h + jnp.concatenate([y_pool, y_attn], axis=-1) @ w_o[i]
        h = h + peer(rmsnorm(h, ln_ffn[i]), w_peer_q[i], peer_subkeys[i], peer_u[i], peer_v[i])
        gate = jax.nn.sigmoid(rmsnorm(h, ln_pe[i]) @ w_pe_gate[i])
        h = h + gate * (p[i] @ w_pe_proj[i])
    return h
```

```python
import functools
import math

import jax
import jax.numpy as jnp
from jax import lax
from jax.experimental import pallas as pl
from jax.experimental.pallas import tpu as pltpu

EPS = 1e-6
ROPE_THETA = 10000.0
POOL_WINDOWS = (2, 4, 8, 16)
POOL_HALO = 16
HEAD_DIM = 64
N_KEYS = 128
PEER_TOPK = 16
N_PEER_HEADS = 8
LANES = 128
VMEM_LIMIT = 56 * 1024 * 1024

BF16 = jnp.bfloat16
F32 = jnp.float32


def _rms(x, gain):
    return x * lax.rsqrt(jnp.mean(x * x, axis=-1, keepdims=True) + EPS) * gain


def _gelu_exact(x):
    return 0.5 * x * (1.0 + lax.erf(x * math.sqrt(0.5)))


def _dot(a, b):
    return jnp.dot(a, b, preferred_element_type=F32)


def _dot_nt(a, b):
    return lax.dot_general(a, b, (((1,), (1,)), ((), ())), preferred_element_type=F32)


def _mix_in_kernel(x_ref, pos_ref, ln_ref, w_in_ref, pool_w_ref, pool_scale_ref,
                   qn_ref, kn_ref, invf_ref, bd_ref,
                   ypool_ref, q_ref, k_ref, v_ref, ext_ref, *, tile):
    s_idx = pl.program_id(1)
    d_pool = ypool_ref.shape[-1]
    d_attn = q_ref.shape[-1]
    group = d_pool // len(POOL_WINDOWS)

    xn = _rms(x_ref[...], ln_ref[...]).astype(BF16)
    z = _dot(xn, w_in_ref[...])

    @pl.when(s_idx == 0)
    def _():
        ext_ref[0:POOL_HALO, :] = jnp.zeros((POOL_HALO, d_pool), F32)

    @pl.when(s_idx != 0)
    def _():
        ext_ref[0:POOL_HALO, :] = ext_ref[tile:tile + POOL_HALO, :]

    zp = z[:, :d_pool]
    ext_ref[POOL_HALO:POOL_HALO + tile, :] = zp
    t_in_seq = s_idx * tile + lax.broadcasted_iota(jnp.int32, (tile, 1), 0)
    for g, w in enumerate(POOL_WINDOWS):
        lo, hi = g * group, (g + 1) * group
        acc = zp[:, lo:hi]
        for k in range(1, w):
            acc = acc + ext_ref[POOL_HALO - k:POOL_HALO - k + tile, lo:hi]
        cnt = jnp.minimum(t_in_seq + 1, w).astype(F32)
        m = acc / cnt - zp[:, lo:hi]
        y = _dot(m.astype(BF16), pool_w_ref[g])
        ypool_ref[:, lo:hi] = (y * pool_scale_ref[:, lo:hi]).astype(ypool_ref.dtype)

    ang = pos_ref[...].astype(F32) * invf_ref[...]
    cos1 = jnp.cos(ang)
    sin1 = jnp.sin(ang)
    lane = lax.broadcasted_iota(jnp.int32, (1, LANES), 1)
    first_half = (lane % HEAD_DIM) < (HEAD_DIM // 2)
    sin1 = jnp.where(first_half, -sin1, sin1)
    reps = d_attn // LANES
    cos = jnp.concatenate([cos1] * reps, axis=1)
    sin = jnp.concatenate([sin1] * reps, axis=1)
    first_half = jnp.concatenate([first_half] * reps, axis=1)

    def norm_rope(t, gain):
        sq = t * t
        hi_part = sq.astype(BF16)
        lo_part = (sq - hi_part.astype(F32)).astype(BF16)
        ss = _dot(hi_part, bd_ref[...]) + _dot(lo_part, bd_ref[...])
        tn = t * lax.rsqrt(ss * (1.0 / HEAD_DIM) + EPS) * gain
        half = HEAD_DIM // 2
        swapped = jnp.where(first_half,
                            pltpu.roll(tn, d_attn - half, 1),
                            pltpu.roll(tn, half, 1))
        return tn * cos + swapped * sin

    zq = z[:, d_pool:d_pool + d_attn]
    zk = z[:, d_pool + d_attn:d_pool + 2 * d_attn]
    scale = 1.0 / math.sqrt(HEAD_DIM)
    q_ref[...] = (norm_rope(zq, qn_ref[...]) * scale).astype(q_ref.dtype)
    k_ref[...] = norm_rope(zk, kn_ref[...]).astype(k_ref.dtype)
    v_ref[...] = z[:, d_pool + 2 * d_attn:].astype(v_ref.dtype)


def _mix_in(x2, pos2, ln_mix, w_in, pool_w, pool_scale, qn_t, kn_t, invf, bd, *, batch, seq, tile):
    T, D = x2.shape
    d_in = w_in.shape[1]
    d_pool = pool_scale.shape[1]
    d_attn = (d_in - d_pool) // 3
    ns = seq // tile
    row = lambda b, s: (b * ns + s, 0)
    const = lambda b, s: (0, 0)
    out_sd = jax.ShapeDtypeStruct((T, d_pool), BF16)
    return pl.pallas_call(
        functools.partial(_mix_in_kernel, tile=tile),
        grid=(batch, ns),
        in_specs=[
            pl.BlockSpec((tile, D), row),
            pl.BlockSpec((tile, 1), row),
            pl.BlockSpec((1, D), const),
            pl.BlockSpec((D, d_in), const),
            pl.BlockSpec(pool_w.shape, lambda b, s: (0, 0, 0)),
            pl.BlockSpec((1, d_pool), const),
            pl.BlockSpec((1, d_attn), const),
            pl.BlockSpec((1, d_attn), const),
            pl.BlockSpec((1, LANES), const),
            pl.BlockSpec((d_attn, d_attn), const),
        ],
        out_specs=[pl.BlockSpec((tile, d_pool), row)] * 4,
        out_shape=[out_sd] * 4,
        scratch_shapes=[pltpu.VMEM((POOL_HALO + tile, d_pool), F32)],
        compiler_params=pltpu.CompilerParams(
            dimension_semantics=("arbitrary", "arbitrary"), vmem_limit_bytes=VMEM_LIMIT),
        name="mix_in",
    )(x2, pos2, ln_mix, w_in, pool_w, pool_scale, qn_t, kn_t, invf, bd)


def _diff_attn_kernel(q_ref, k_ref, v_ref, lq1_ref, lk1_ref, lq2_ref, lk2_ref, subln_ref,
                      o_ref, *, tq, lam_init):
    qi = pl.program_id(2)
    q = q_ref[...]
    lane = lax.broadcasted_iota(jnp.int32, (1, 2 * HEAD_DIM), 1)
    zero = jnp.zeros_like(q)
    qs = jnp.concatenate([jnp.where(lane < HEAD_DIM, q, zero),
                          jnp.where(lane >= HEAD_DIM, q, zero)], axis=0)

    def step(k_t, v_t, carry, masked):
        m_prev, l_prev, acc = carry
        s = _dot_nt(qs, k_t)
        if masked:
            r = lax.broadcasted_iota(jnp.int32, (tq, tq), 0)
            c = lax.broadcasted_iota(jnp.int32, (tq, tq), 1)
            keep = jnp.concatenate([c <= r, c <= r], axis=0)
            s = jnp.where(keep, s, -jnp.inf)
        m_new = jnp.maximum(m_prev, jnp.max(s, axis=1, keepdims=True))
        alpha = jnp.exp(m_prev - m_new)
        p = jnp.exp(s - m_new)
        l_new = alpha * l_prev + jnp.sum(p, axis=1, keepdims=True)
        acc = alpha * acc + _dot(p.astype(BF16), v_t)
        return m_new, l_new, acc

    def body(j, carry):
        off = pl.multiple_of(j * tq, tq)
        return step(k_ref[pl.ds(off, tq), :], v_ref[pl.ds(off, tq), :], carry, False)

    init = (jnp.full((2 * tq, 1), -jnp.inf, F32), jnp.zeros((2 * tq, 1), F32),
            jnp.zeros((2 * tq, 2 * HEAD_DIM), F32))
    carry = lax.fori_loop(0, qi, body, init)
    off = pl.multiple_of(qi * tq, tq)
    _, l, acc = step(k_ref[pl.ds(off, tq), :], v_ref[pl.ds(off, tq), :], carry, True)

    lam = (jnp.exp(jnp.sum(lq1_ref[...] * lk1_ref[...], axis=1, keepdims=True))
           - jnp.exp(jnp.sum(lq2_ref[...] * lk2_ref[...], axis=1, keepdims=True)) + lam_init)
    o = acc[:tq] / l[:tq] - lam * (acc[tq:] / l[tq:])
    o_ref[...] = (_rms(o, subln_ref[...]) * (1.0 - lam_init)).astype(o_ref.dtype)


def _diff_attn(q, k, v, lq1, lk1, lq2, lk2, subln, *, batch, seq, tq, lam_init):
    T, d_attn = q.shape
    heads = d_attn // (2 * HEAD_DIM)
    nq = seq // tq
    hw = 2 * HEAD_DIM
    vec = pl.BlockSpec((1, HEAD_DIM), lambda b, h, i: (0, 0))
    return pl.pallas_call(
        functools.partial(_diff_attn_kernel, tq=tq, lam_init=lam_init),
        grid=(batch, heads, nq),
        in_specs=[
            pl.BlockSpec((tq, hw), lambda b, h, i: (b * nq + i, h)),
            pl.BlockSpec((seq, hw), lambda b, h, i: (b, h)),
            pl.BlockSpec((seq, hw), lambda b, h, i: (b, h)),
            vec, vec, vec, vec,
            pl.BlockSpec((1, hw), lambda b, h, i: (0, 0)),
        ],
        out_specs=pl.BlockSpec((tq, hw), lambda b, h, i: (b * nq + i, h)),
        out_shape=jax.ShapeDtypeStruct((T, d_attn), BF16),
        compiler_params=pltpu.CompilerParams(
            dimension_semantics=("arbitrary", "arbitrary", "arbitrary"),
            vmem_limit_bytes=VMEM_LIMIT),
        name="diff_attn",
    )(q, k, v, lq1, lk1, lq2, lk2, subln)


def _peer_front_kernel(x_ref, ypool_ref, o_ref, wo_ref, ln_ref, wq_ref, sk_ref,
                       h1_ref, xn_ref, s1_ref, s2_ref, ea_ref, eb_ref, tau_ref,
                       a_ref, b_ref, cand_ref):
    d_pool = ypool_ref.shape[-1]
    tile = x_ref.shape[0]
    h1 = (x_ref[...] + _dot(ypool_ref[...], wo_ref[0:d_pool, :])
          + _dot(o_ref[...], wo_ref[d_pool:, :]))
    h1_ref[...] = h1
    xn = _rms(h1, ln_ref[...]).astype(BF16)
    xn_ref[...] = xn
    qp = _dot(xn, wq_ref[...]).astype(BF16)

    neg = jnp.float32(-jnp.inf)

    def extract_sorted(x, dst_ref, n):
        for r in range(n):
            m = jnp.max(x, axis=0, keepdims=True)
            dst_ref[r:r + 1, :] = m
            x = jnp.where(x == m, neg, x)

    for h in range(N_PEER_HEADS):
        c0 = (2 * h) * N_KEYS
        s1 = _dot_nt(sk_ref[h, 0], qp[:, c0:c0 + N_KEYS])
        s2 = _dot_nt(sk_ref[h, 1], qp[:, c0 + N_KEYS:c0 + 2 * N_KEYS])
        extract_sorted(s1, a_ref, PEER_TOPK)
        extract_sorted(s2, b_ref, PEER_TOPK)
        cand_ref[0:PEER_TOPK, :] = a_ref[0:1, :] + b_ref[...]
        for p in range(1, PEER_TOPK):
            r0 = PEER_TOPK + 8 * (p - 1)
            cand_ref[r0:r0 + 8, :] = a_ref[p:p + 1, :] + b_ref[0:8, :]
        cand = cand_ref[...]
        c = cand
        tau = None
        for r in range(PEER_TOPK):
            tau = jnp.max(c, axis=0, keepdims=True)
            c = jnp.where(c == tau, neg, c)
        top = a_ref[0:1, :] + b_ref[0:1, :]
        zsum = jnp.sum(jnp.where(cand >= tau, jnp.exp(cand - top), 0.0), axis=0, keepdims=True)
        s1_ref[h] = s1
        s2_ref[h] = s2
        ea_ref[h] = jnp.exp(s1 - a_ref[0:1, :]) / zsum
        eb_ref[h] = jnp.exp(s2 - b_ref[0:1, :])
        tau_ref[h] = jnp.broadcast_to(tau, (8, tile))


def _peer_front(x2, ypool, o, w_o, ln_ffn, w_q, subkeys, *, tile):
    T, D = x2.shape
    d_pool = ypool.shape[1]
    nh = subkeys.shape[0]
    row = lambda i: (i, 0)
    const = lambda i: (0, 0)
    tok = lambda i: (0, 0, i)
    big = jax.ShapeDtypeStruct((nh, N_KEYS, T), F32)
    n_cand = PEER_TOPK + 8 * (PEER_TOPK - 1)
    return pl.pallas_call(
        _peer_front_kernel,
        grid=(T // tile,),
        in_specs=[
            pl.BlockSpec((tile, D), row),
            pl.BlockSpec((tile, d_pool), row),
            pl.BlockSpec((tile, o.shape[1]), row),
            pl.BlockSpec(w_o.shape, const),
            pl.BlockSpec((1, D), const),
            pl.BlockSpec(w_q.shape, const),
            pl.BlockSpec(subkeys.shape, lambda i: (0, 0, 0, 0)),
        ],
        out_specs=[
            pl.BlockSpec((tile, D), row),
            pl.BlockSpec((tile, D), row),
            pl.BlockSpec((nh, N_KEYS, tile), tok),
            pl.BlockSpec((nh, N_KEYS, tile), tok),
            pl.BlockSpec((nh, N_KEYS, tile), tok),
            pl.BlockSpec((nh, N_KEYS, tile), tok),
            pl.BlockSpec((nh, 8, tile), tok),
        ],
        out_shape=[
            jax.ShapeDtypeStruct((T, D), F32),
            jax.ShapeDtypeStruct((T, D), BF16),
            big, big, big, big,
            jax.ShapeDtypeStruct((nh, 8, T), F32),
        ],
        scratch_shapes=[
            pltpu.VMEM((PEER_TOPK, tile), F32),
            pltpu.VMEM((PEER_TOPK, tile), F32),
            pltpu.VMEM((n_cand, tile), F32),
        ],
        compiler_params=pltpu.CompilerParams(
            dimension_semantics=("arbitrary",), vmem_limit_bytes=VMEM_LIMIT),
        name="peer_front",
    )(x2, ypool, o, w_o, ln_ffn, w_q, subkeys)


def _peer_dense_kernel(xn_ref, h1_ref, u_ref, vt_ref, s1_ref, s2_ref, ea_ref, eb_ref, tau_ref,
                       out_ref, acc_ref, hid_ref, *, keys_per_block):
    e_idx = pl.program_id(1)
    n_e = pl.num_programs(1)
    tile = xn_ref.shape[0]

    @pl.when(e_idx == 0)
    def _():
        acc_ref[...] = jnp.zeros_like(acc_ref)

    a_t = _dot_nt(u_ref[...], xn_ref[...])
    for ii in range(keys_per_block):
        i = e_idx * keys_per_block + ii
        gate = jnp.zeros((N_KEYS, tile), F32)
        for h in range(N_PEER_HEADS):
            pair = s2_ref[h] + s1_ref[h, pl.ds(i, 1), :]
            w = eb_ref[h] * ea_ref[h, pl.ds(i, 1), :]
            gate = gate + jnp.where(pair >= tau_ref[h, 0:1, :], w, 0.0)
        a_i = a_t[ii * N_KEYS:(ii + 1) * N_KEYS, :]
        hid = _gelu_exact(a_i) * gate
        hid_ref[ii * N_KEYS:(ii + 1) * N_KEYS, :] = hid.astype(hid_ref.dtype)
    acc_ref[...] += _dot(vt_ref[...], hid_ref[...])

    @pl.when(e_idx == n_e - 1)
    def _():
        out_ref[...] = h1_ref[...] + acc_ref[...].T


def _peer_dense(xn, h1, u, vt, s1, s2, ea, eb, tau, *, tile, keys_per_block):
    T, D = xn.shape
    n_exp = u.shape[0]
    eblk = keys_per_block * N_KEYS
    nh = s1.shape[0]
    row = lambda t, e: (t, 0)
    tok = lambda t, e: (0, 0, t)
    return pl.pallas_call(
        functools.partial(_peer_dense_kernel, keys_per_block=keys_per_block),
        grid=(T // tile, n_exp // eblk),
        in_specs=[
            pl.BlockSpec((tile, D), row),
            pl.BlockSpec((tile, D), row),
            pl.BlockSpec((eblk, D), lambda t, e: (e, 0)),
            pl.BlockSpec((D, eblk), lambda t, e: (0, e)),
            pl.BlockSpec((nh, N_KEYS, tile), tok),
            pl.BlockSpec((nh, N_KEYS, tile), tok),
            pl.BlockSpec((nh, N_KEYS, tile), tok),
            pl.BlockSpec((nh, N_KEYS, tile), tok),
            pl.BlockSpec((nh, 8, tile), tok),
        ],
        out_specs=pl.BlockSpec((tile, D), row),
        out_shape=jax.ShapeDtypeStruct((T, D), F32),
        scratch_shapes=[pltpu.VMEM((D, tile), F32), pltpu.VMEM((eblk, tile), BF16)],
        compiler_params=pltpu.CompilerParams(
            dimension_semantics=("arbitrary", "arbitrary"), vmem_limit_bytes=VMEM_LIMIT),
        name="peer_dense",
    )(xn, h1, u, vt, s1, s2, ea, eb, tau)


def _ple_kernel(h_ref, p_ref, ln_ref, wg_ref, wp_ref, out_ref):
    h = h_ref[...]
    gate = jax.nn.sigmoid(_dot(_rms(h, ln_ref[...]).astype(BF16), wg_ref[...]))
    out_ref[...] = h + gate * _dot(p_ref[...].astype(BF16), wp_ref[...])


def _ple(h2, p2, ln_pe, w_gate, w_proj, *, tile):
    T, D = h2.shape
    row = lambda i: (i, 0)
    const = lambda i: (0, 0)
    return pl.pallas_call(
        _ple_kernel,
        grid=(T // tile,),
        in_specs=[
            pl.BlockSpec((tile, D), row),
            pl.BlockSpec((tile, p2.shape[1]), row),
            pl.BlockSpec((1, D), const),
            pl.BlockSpec(w_gate.shape, const),
            pl.BlockSpec(w_proj.shape, const),
        ],
        out_specs=pl.BlockSpec((tile, D), row),
        out_shape=jax.ShapeDtypeStruct((T, D), F32),
        compiler_params=pltpu.CompilerParams(
            dimension_semantics=("arbitrary",), vmem_limit_bytes=VMEM_LIMIT),
        name="ple",
    )(h2, p2, ln_pe, w_gate, w_proj)


def _pick_tile(n, pref):
    t = min(pref, n)
    assert n % t == 0, (n, t)
    return t


def kernel(x, p, positions, ln_mix, w_in, pool_w, pool_scale, q_norm, k_norm,
           lambda_q1, lambda_k1, lambda_q2, lambda_k2, subln, w_o, ln_ffn,
           w_peer_q, peer_subkeys, peer_u, peer_v, ln_pe, w_pe_gate, w_pe_proj):
    B, S, D = x.shape
    T = B * S
    depth = w_in.shape[0]
    d_pool = pool_scale.shape[-1]
    d_attn = (w_in.shape[-1] - d_pool) // 3
    n_groups = d_attn // HEAD_DIM

    lane = jnp.arange(LANES)
    invf = (ROPE_THETA ** (-(2.0 * (lane % (HEAD_DIM // 2))).astype(F32) / HEAD_DIM)).reshape(1, LANES)
    gid = jnp.arange(d_attn) // HEAD_DIM
    bd = (gid[:, None] == gid[None, :]).astype(BF16)

    pos2 = positions.reshape(T, 1)
    h = x.reshape(T, D)
    for i in range(depth):
        lam_init = 0.8 - 0.6 * math.exp(-0.3 * i)
        ypool, q, k, v = _mix_in(
            h, pos2, ln_mix[i][None], w_in[i].astype(BF16), pool_w[i].astype(BF16),
            pool_scale[i][None], jnp.tile(q_norm[i], n_groups)[None],
            jnp.tile(k_norm[i], n_groups)[None], invf, bd,
            batch=B, seq=S, tile=_pick_tile(S, 512))
        o = _diff_attn(q, k, v, lambda_q1[i][None], lambda_k1[i][None], lambda_q2[i][None],
                       lambda_k2[i][None], subln[i][None],
                       batch=B, seq=S, tq=_pick_tile(S, 512), lam_init=lam_init)
        h1, xn, s1, s2, ea, eb, tau = _peer_front(
            h, ypool, o, w_o[i].astype(BF16), ln_ffn[i][None], w_peer_q[i].astype(BF16),
            peer_subkeys[i].astype(BF16), tile=_pick_tile(T, 256))
        h2 = _peer_dense(xn, h1, peer_u[i].astype(BF16), peer_v[i].T.astype(BF16),
                         s1, s2, ea, eb, tau, tile=_pick_tile(T, 512), keys_per_block=4)
        h = _ple(h2, p[i].reshape(T, -1), ln_pe[i][None], w_pe_gate[i].astype(BF16),
                 w_pe_proj[i].astype(BF16), tile=_pick_tile(T, 512))
    return h.reshape(B, S, D)
```

```python
import functools
import math

import jax
import jax.numpy as jnp
from jax import lax
from jax.experimental import pallas as pl
from jax.experimental.pallas import tpu as pltpu

EPS = 1e-6
ROPE_THETA = 10000.0
POOL_WINDOWS = (2, 4, 8, 16)
POOL_HALO = 16
HEAD_DIM = 64
N_KEYS = 128
PEER_TOPK = 16
N_PEER_HEADS = 8
LANES = 128
VMEM_LIMIT = 56 * 1024 * 1024

BF16 = jnp.bfloat16
F32 = jnp.float32


def _rms(x, gain):
    return x * lax.rsqrt(jnp.mean(x * x, axis=-1, keepdims=True) + EPS) * gain


def _gelu_exact(x):
    return 0.5 * x * (1.0 + lax.erf(x * math.sqrt(0.5)))


def _dot(a, b):
    return jnp.dot(a, b, preferred_element_type=F32)


def _dot_nt(a, b):
    return lax.dot_general(a, b, (((1,), (1,)), ((), ())), preferred_element_type=F32)


def _mix_in_kernel(x_ref, pos_ref, ln_ref, w_in_ref, pool_w_ref, pool_scale_ref,
                   qn_ref, kn_ref, invf_ref, bd_ref,
                   ypool_ref, q_ref, k_ref, v_ref, ext_ref, *, tile):
    s_idx = pl.program_id(1)
    d_pool = ypool_ref.shape[-1]
    d_attn = q_ref.shape[-1]
    group = d_pool // len(POOL_WINDOWS)

    xn = _rms(x_ref[...], ln_ref[...]).astype(BF16)
    z = _dot(xn, w_in_ref[...])

    @pl.when(s_idx == 0)
    def _():
        ext_ref[0:POOL_HALO, :] = jnp.zeros((POOL_HALO, d_pool), F32)

    @pl.when(s_idx != 0)
    def _():
        ext_ref[0:POOL_HALO, :] = ext_ref[tile:tile + POOL_HALO, :]

    zp = z[:, :d_pool]
    ext_ref[POOL_HALO:POOL_HALO + tile, :] = zp
    t_in_seq = s_idx * tile + lax.broadcasted_iota(jnp.int32, (tile, 1), 0)
    for g, w in enumerate(POOL_WINDOWS):
        lo, hi = g * group, (g + 1) * group
        acc = zp[:, lo:hi]
        for k in range(1, w):
            acc = acc + ext_ref[POOL_HALO - k:POOL_HALO - k + tile, lo:hi]
        cnt = jnp.minimum(t_in_seq + 1, w).astype(F32)
        m = acc / cnt - zp[:, lo:hi]
        y = _dot(m.astype(BF16), pool_w_ref[g])
        ypool_ref[:, lo:hi] = (y * pool_scale_ref[:, lo:hi]).astype(ypool_ref.dtype)

    ang = pos_ref[...].astype(F32) * invf_ref[...]
    cos1 = jnp.cos(ang)
    sin1 = jnp.sin(ang)
    lane = lax.broadcasted_iota(jnp.int32, (1, LANES), 1)
    first_half = (lane % HEAD_DIM) < (HEAD_DIM // 2)
    sin1 = jnp.where(first_half, -sin1, sin1)
    reps = d_attn // LANES
    cos = jnp.concatenate([cos1] * reps, axis=1)
    sin = jnp.concatenate([sin1] * reps, axis=1)
    first_half = jnp.concatenate([first_half] * reps, axis=1)

    def norm_rope(t, gain):
        sq = t * t
        hi_part = sq.astype(BF16)
        lo_part = (sq - hi_part.astype(F32)).astype(BF16)
        ss = _dot(hi_part, bd_ref[...]) + _dot(lo_part, bd_ref[...])
        tn = t * lax.rsqrt(ss * (1.0 / HEAD_DIM) + EPS) * gain
        half = HEAD_DIM // 2
        swapped = jnp.where(first_half,
                            pltpu.roll(tn, d_attn - half, 1),
                            pltpu.roll(tn, half, 1))
        return tn * cos + swapped * sin

    zq = z[:, d_pool:d_pool + d_attn]
    zk = z[:, d_pool + d_attn:d_pool + 2 * d_attn]
    scale = 1.0 / math.sqrt(HEAD_DIM)
    q_ref[...] = (norm_rope(zq, qn_ref[...]) * scale).astype(q_ref.dtype)
    k_ref[...] = norm_rope(zk, kn_ref[...]).astype(k_ref.dtype)
    v_ref[...] = z[:, d_pool + 2 * d_attn:].astype(v_ref.dtype)


def _mix_in(x2, pos2, ln_mix, w_in, pool_w, pool_scale, qn_t, kn_t, invf, bd, *, batch, seq, tile):
    T, D = x2.shape
    d_in = w_in.shape[1]
    d_pool = pool_scale.shape[1]
    d_attn = (d_in - d_pool) // 3
    ns = seq // tile
    row = lambda b, s: (b * ns + s, 0)
    const = lambda b, s: (0, 0)
    out_sd = jax.ShapeDtypeStruct((T, d_pool), BF16)
    return pl.pallas_call(
        functools.partial(_mix_in_kernel, tile=tile),
        grid=(batch, ns),
        in_specs=[
            pl.BlockSpec((tile, D), row),
            pl.BlockSpec((tile, 1), row),
            pl.BlockSpec((1, D), const),
            pl.BlockSpec((D, d_in), const),
            pl.BlockSpec(pool_w.shape, lambda b, s: (0, 0, 0)),
            pl.BlockSpec((1, d_pool), const),
            pl.BlockSpec((1, d_attn), const),
            pl.BlockSpec((1, d_attn), const),
            pl.BlockSpec((1, LANES), const),
            pl.BlockSpec((d_attn, d_attn), const),
        ],
        out_specs=[pl.BlockSpec((tile, d_pool), row)] * 4,
        out_shape=[out_sd] * 4,
        scratch_shapes=[pltpu.VMEM((POOL_HALO + tile, d_pool), F32)],
        compiler_params=pltpu.CompilerParams(
            dimension_semantics=("arbitrary", "arbitrary"), vmem_limit_bytes=VMEM_LIMIT),
        name="mix_in",
    )(x2, pos2, ln_mix, w_in, pool_w, pool_scale, qn_t, kn_t, invf, bd)


def _diff_attn_kernel(q_ref, k_ref, v_ref, lq1_ref, lk1_ref, lq2_ref, lk2_ref, subln_ref,
                      o_ref, *, tq, lam_init):
    qi = pl.program_id(2)
    q = q_ref[...]
    lane = lax.broadcasted_iota(jnp.int32, (1, 2 * HEAD_DIM), 1)
    zero = jnp.zeros_like(q)
    qs = jnp.concatenate([jnp.where(lane < HEAD_DIM, q, zero),
                          jnp.where(lane >= HEAD_DIM, q, zero)], axis=0)

    def step(k_t, v_t, carry, masked):
        m_prev, l_prev, acc = carry
        s = _dot_nt(qs, k_t)
        if masked:
            r = lax.broadcasted_iota(jnp.int32, (tq, tq), 0)
            c = lax.broadcasted_iota(jnp.int32, (tq, tq), 1)
            keep = jnp.concatenate([c <= r, c <= r], axis=0)
            s = jnp.where(keep, s, -jnp.inf)
        m_new = jnp.maximum(m_prev, jnp.max(s, axis=1, keepdims=True))
        alpha = jnp.exp(m_prev - m_new)
        p = jnp.exp(s - m_new)
        l_new = alpha * l_prev + jnp.sum(p, axis=1, keepdims=True)
        acc = alpha * acc + _dot(p.astype(BF16), v_t)
        return m_new, l_new, acc

    def body(j, carry):
        off = pl.multiple_of(j * tq, tq)
        return step(k_ref[pl.ds(off, tq), :], v_ref[pl.ds(off, tq), :], carry, False)

    init = (jnp.full((2 * tq, 1), -jnp.inf, F32), jnp.zeros((2 * tq, 1), F32),
            jnp.zeros((2 * tq, 2 * HEAD_DIM), F32))
    carry = lax.fori_loop(0, qi, body, init)
    off = pl.multiple_of(qi * tq, tq)
    _, l, acc = step(k_ref[pl.ds(off, tq), :], v_ref[pl.ds(off, tq), :], carry, True)

    lam = (jnp.exp(jnp.sum(lq1_ref[...] * lk1_ref[...], axis=1, keepdims=True))
           - jnp.exp(jnp.sum(lq2_ref[...] * lk2_ref[...], axis=1, keepdims=True)) + lam_init)
    o = acc[:tq] / l[:tq] - lam * (acc[tq:] / l[tq:])
    o_ref[...] = (_rms(o, subln_ref[...]) * (1.0 - lam_init)).astype(o_ref.dtype)


def _diff_attn(q, k, v, lq1, lk1, lq2, lk2, subln, *, batch, seq, tq, lam_init):
    T, d_attn = q.shape
    heads = d_attn // (2 * HEAD_DIM)
    nq = seq // tq
    hw = 2 * HEAD_DIM
    vec = pl.BlockSpec((1, HEAD_DIM), lambda b, h, i: (0, 0))
    return pl.pallas_call(
        functools.partial(_diff_attn_kernel, tq=tq, lam_init=lam_init),
        grid=(batch, heads, nq),
        in_specs=[
            pl.BlockSpec((tq, hw), lambda b, h, i: (b * nq + i, h)),
            pl.BlockSpec((seq, hw), lambda b, h, i: (b, h)),
            pl.BlockSpec((seq, hw), lambda b, h, i: (b, h)),
            vec, vec, vec, vec,
            pl.BlockSpec((1, hw), lambda b, h, i: (0, 0)),
        ],
        out_specs=pl.BlockSpec((tq, hw), lambda b, h, i: (b * nq + i, h)),
        out_shape=jax.ShapeDtypeStruct((T, d_attn), BF16),
        compiler_params=pltpu.CompilerParams(
            dimension_semantics=("arbitrary", "arbitrary", "arbitrary"),
            vmem_limit_bytes=VMEM_LIMIT),
        name="diff_attn",
    )(q, k, v, lq1, lk1, lq2, lk2, subln)


def _peer_front_kernel(x_ref, ypool_ref, o_ref, wo_ref, ln_ref, wq_ref, sk_ref,
                       h1_ref, xn_ref, rank2_ref, eb_ref, quota_ref, ea_ref,
                       a_ref, b_ref, cand_ref):
    d_pool = ypool_ref.shape[-1]
    h1 = (x_ref[...] + _dot(ypool_ref[...], wo_ref[0:d_pool, :])
          + _dot(o_ref[...], wo_ref[d_pool:, :]))
    h1_ref[...] = h1
    xn = _rms(h1, ln_ref[...]).astype(BF16)
    xn_ref[...] = xn
    qp = _dot(xn, wq_ref[...]).astype(BF16)

    neg = jnp.float32(-jnp.inf)
    k_top = float(PEER_TOPK)

    def extract_sorted(x, dst_ref):
        rank = jnp.full(x.shape, k_top, F32)
        for r in range(PEER_TOPK):
            m = jnp.max(x, axis=0, keepdims=True)
            dst_ref[r:r + 1, :] = m
            hit = x == m
            rank = jnp.where(hit, float(r), rank)
            x = jnp.where(hit, neg, x)
        return rank

    for h in range(N_PEER_HEADS):
        c0 = (2 * h) * N_KEYS
        s1 = _dot_nt(sk_ref[h, 0], qp[:, c0:c0 + N_KEYS])
        s2 = _dot_nt(sk_ref[h, 1], qp[:, c0 + N_KEYS:c0 + 2 * N_KEYS])
        rank1 = extract_sorted(s1, a_ref)
        rank2 = extract_sorted(s2, b_ref)
        a = a_ref[...]
        b = b_ref[...]
        cand_ref[0:PEER_TOPK, :] = a[0:1, :] + b
        for p in range(1, PEER_TOPK):
            r0 = PEER_TOPK + 8 * (p - 1)
            cand_ref[r0:r0 + 8, :] = a[p:p + 1, :] + b[0:8, :]
        cand = cand_ref[...]
        c = cand
        tau = None
        for r in range(PEER_TOPK):
            tau = jnp.max(c, axis=0, keepdims=True)
            c = jnp.where(c == tau, neg, c)
        top = a[0:1, :] + b[0:1, :]
        zsum = jnp.sum(jnp.where(cand >= tau, jnp.exp(cand - top), 0.0), axis=0, keepdims=True)
        quota_sorted = jnp.zeros(a.shape, F32)
        for q in range(PEER_TOPK):
            quota_sorted = quota_sorted + jnp.where(a + b[q:q + 1, :] >= tau, 1.0, 0.0)
        quota = jnp.zeros(s1.shape, F32)
        for p in range(PEER_TOPK):
            quota = jnp.where(rank1 == float(p), quota_sorted[p:p + 1, :], quota)
        rank2_ref[h] = rank2.astype(rank2_ref.dtype)
        eb_ref[h] = jnp.exp(s2 - b[0:1, :]).astype(eb_ref.dtype)
        quota_ref[h] = quota
        ea_ref[h] = jnp.exp(s1 - a[0:1, :]) / zsum


def _peer_front(x2, ypool, o, w_o, ln_ffn, w_q, subkeys, *, tile):
    T, D = x2.shape
    d_pool = ypool.shape[1]
    nh = subkeys.shape[0]
    row = lambda i: (i, 0)
    const = lambda i: (0, 0)
    tok = pl.BlockSpec((nh, N_KEYS, tile), lambda i: (0, 0, i))
    n_cand = PEER_TOPK + 8 * (PEER_TOPK - 1)
    return pl.pallas_call(
        _peer_front_kernel,
        grid=(T // tile,),
        in_specs=[
            pl.BlockSpec((tile, D), row),
            pl.BlockSpec((tile, d_pool), row),
            pl.BlockSpec((tile, o.shape[1]), row),
            pl.BlockSpec(w_o.shape, const),
            pl.BlockSpec((1, D), const),
            pl.BlockSpec(w_q.shape, const),
            pl.BlockSpec(subkeys.shape, lambda i: (0, 0, 0, 0)),
        ],
        out_specs=[pl.BlockSpec((tile, D), row), pl.BlockSpec((tile, D), row), tok, tok, tok, tok],
        out_shape=[
            jax.ShapeDtypeStruct((T, D), F32),
            jax.ShapeDtypeStruct((T, D), BF16),
            jax.ShapeDtypeStruct((nh, N_KEYS, T), BF16),
            jax.ShapeDtypeStruct((nh, N_KEYS, T), BF16),
            jax.ShapeDtypeStruct((nh, N_KEYS, T), F32),
            jax.ShapeDtypeStruct((nh, N_KEYS, T), F32),
        ],
        scratch_shapes=[
            pltpu.VMEM((PEER_TOPK, tile), F32),
            pltpu.VMEM((PEER_TOPK, tile), F32),
            pltpu.VMEM((n_cand, tile), F32),
        ],
        compiler_params=pltpu.CompilerParams(
            dimension_semantics=("arbitrary",), vmem_limit_bytes=VMEM_LIMIT),
        name="peer_front",
    )(x2, ypool, o, w_o, ln_ffn, w_q, subkeys)


ROWS_BF16 = 16


def _peer_dense_kernel(xn_ref, h1_ref, u_ref, vt_ref, rank2_ref, eb_ref, quota_ref, ea_ref,
                       out_ref, acc_ref, hid_ref, *, keys_per_block):
    e_idx = pl.program_id(1)
    n_blocks = pl.num_programs(1) - 1
    tile = xn_ref.shape[0]
    n_chunks = N_KEYS // ROWS_BF16
    slot = e_idx % 2

    @pl.when(e_idx == 0)
    def _():
        acc_ref[...] = jnp.zeros_like(acc_ref)
        hid_ref[1] = jnp.zeros(hid_ref.shape[1:], hid_ref.dtype)

    @pl.when(e_idx < n_blocks)
    def _():
        acc_ref[...] += _dot(vt_ref[...], hid_ref[1 - slot])
        a_t = _dot_nt(u_ref[...], xn_ref[...])
        for ii in range(keys_per_block):
            i = e_idx * keys_per_block + ii
            gates = [jnp.zeros((ROWS_BF16, tile), BF16)] * n_chunks
            for h in range(N_PEER_HEADS):
                quota = jnp.broadcast_to(quota_ref[h, pl.ds(i, 1), :], (ROWS_BF16, tile)).astype(BF16)
                ea = jnp.broadcast_to(ea_ref[h, pl.ds(i, 1), :], (ROWS_BF16, tile)).astype(BF16)
                for c in range(n_chunks):
                    rows = slice(c * ROWS_BF16, (c + 1) * ROWS_BF16)
                    w = eb_ref[h, rows, :] * ea
                    gates[c] = gates[c] + jnp.where(rank2_ref[h, rows, :] < quota, w,
                                                    jnp.zeros_like(w))
            for c in range(n_chunks):
                r0 = ii * N_KEYS + c * ROWS_BF16
                act = _gelu_exact(a_t[r0:r0 + ROWS_BF16, :]).astype(BF16)
                hid_ref[slot, r0:r0 + ROWS_BF16, :] = act * gates[c]

    @pl.when(e_idx == n_blocks)
    def _():
        acc = acc_ref[...] + _dot(vt_ref[...], hid_ref[1 - slot])
        out_ref[...] = h1_ref[...] + acc.T


def _peer_dense(xn, h1, u, vt, rank2, eb, quota, ea, *, tile, keys_per_block):
    T, D = xn.shape
    n_exp = u.shape[0]
    eblk = keys_per_block * N_KEYS
    nh = rank2.shape[0]
    n_blocks = n_exp // eblk
    row = lambda t, e: (t, 0)
    tok = pl.BlockSpec((nh, N_KEYS, tile), lambda t, e: (0, 0, t))
    return pl.pallas_call(
        functools.partial(_peer_dense_kernel, keys_per_block=keys_per_block),
        grid=(T // tile, n_blocks + 1),
        in_specs=[
            pl.BlockSpec((tile, D), row),
            pl.BlockSpec((tile, D), row),
            pl.BlockSpec((eblk, D), lambda t, e: (jnp.minimum(e, n_blocks - 1), 0)),
            pl.BlockSpec((D, eblk), lambda t, e: (0, jnp.maximum(e - 1, 0))),
            tok, tok, tok, tok,
        ],
        out_specs=pl.BlockSpec((tile, D), row),
        out_shape=jax.ShapeDtypeStruct((T, D), F32),
        scratch_shapes=[pltpu.VMEM((D, tile), F32), pltpu.VMEM((2, eblk, tile), BF16)],
        compiler_params=pltpu.CompilerParams(
            dimension_semantics=("arbitrary", "arbitrary"), vmem_limit_bytes=VMEM_LIMIT),
        name="peer_dense",
    )(xn, h1, u, vt, rank2, eb, quota, ea)


def _ple_kernel(h_ref, p_ref, ln_ref, wg_ref, wp_ref, out_ref):
    h = h_ref[...]
    gate = jax.nn.sigmoid(_dot(_rms(h, ln_ref[...]).astype(BF16), wg_ref[...]))
    out_ref[...] = h + gate * _dot(p_ref[...].astype(BF16), wp_ref[...])


def _ple(h2, p2, ln_pe, w_gate, w_proj, *, tile):
    T, D = h2.shape
    row = lambda i: (i, 0)
    const = lambda i: (0, 0)
    return pl.pallas_call(
        _ple_kernel,
        grid=(T // tile,),
        in_specs=[
            pl.BlockSpec((tile, D), row),
            pl.BlockSpec((tile, p2.shape[1]), row),
            pl.BlockSpec((1, D), const),
            pl.BlockSpec(w_gate.shape, const),
            pl.BlockSpec(w_proj.shape, const),
        ],
        out_specs=pl.BlockSpec((tile, D), row),
        out_shape=jax.ShapeDtypeStruct((T, D), F32),
        compiler_params=pltpu.CompilerParams(
            dimension_semantics=("arbitrary",), vmem_limit_bytes=VMEM_LIMIT),
        name="ple",
    )(h2, p2, ln_pe, w_gate, w_proj)


def _pick_tile(n, pref):
    t = min(pref, n)
    assert n % t == 0, (n, t)
    return t


def kernel(x, p, positions, ln_mix, w_in, pool_w, pool_scale, q_norm, k_norm,
           lambda_q1, lambda_k1, lambda_q2, lambda_k2, subln, w_o, ln_ffn,
           w_peer_q, peer_subkeys, peer_u, peer_v, ln_pe, w_pe_gate, w_pe_proj):
    B, S, D = x.shape
    T = B * S
    depth = w_in.shape[0]
    d_pool = pool_scale.shape[-1]
    d_attn = (w_in.shape[-1] - d_pool) // 3
    n_groups = d_attn // HEAD_DIM

    lane = jnp.arange(LANES)
    invf = (ROPE_THETA ** (-(2.0 * (lane % (HEAD_DIM // 2))).astype(F32) / HEAD_DIM)).reshape(1, LANES)
    gid = jnp.arange(d_attn) // HEAD_DIM
    bd = (gid[:, None] == gid[None, :]).astype(BF16)

    pos2 = positions.reshape(T, 1)
    h = x.reshape(T, D)
    for i in range(depth):
        lam_init = 0.8 - 0.6 * math.exp(-0.3 * i)
        ypool, q, k, v = _mix_in(
            h, pos2, ln_mix[i][None], w_in[i].astype(BF16), pool_w[i].astype(BF16),
            pool_scale[i][None], jnp.tile(q_norm[i], n_groups)[None],
            jnp.tile(k_norm[i], n_groups)[None], invf, bd,
            batch=B, seq=S, tile=_pick_tile(S, 512))
        o = _diff_attn(q, k, v, lambda_q1[i][None], lambda_k1[i][None], lambda_q2[i][None],
                       lambda_k2[i][None], subln[i][None],
                       batch=B, seq=S, tq=_pick_tile(S, 512), lam_init=lam_init)
        h1, xn, rank2, eb, quota, ea = _peer_front(
            h, ypool, o, w_o[i].astype(BF16), ln_ffn[i][None], w_peer_q[i].astype(BF16),
            peer_subkeys[i].astype(BF16), tile=_pick_tile(T, 256))
        h2 = _peer_dense(xn, h1, peer_u[i].astype(BF16), peer_v[i].T.astype(BF16),
                         rank2, eb, quota, ea, tile=_pick_tile(T, 512), keys_per_block=4)
        h = _ple(h2, p[i].reshape(T, -1), ln_pe[i][None], w_pe_gate[i].astype(BF16),
                 w_pe_proj[i].astype(BF16), tile=_pick_tile(T, 512))
    return h.reshape(B, S, D)
```

```python
import functools
import math

import jax
import jax.numpy as jnp
from jax import lax
from jax.experimental import pallas as pl
from jax.experimental.pallas import tpu as pltpu

EPS = 1e-6
ROPE_THETA = 10000.0
POOL_WINDOWS = (2, 4, 8, 16)
POOL_HALO = 16
HEAD_DIM = 64
QK_SCALE = math.log2(math.e) / math.sqrt(HEAD_DIM)
EXP2_SAFE_RANGE = 60.0
N_KEYS = 128
PEER_TOPK = 16
N_PEER_HEADS = 8
LANES = 128
ROWS_F32 = 8
ROWS_BF16 = 16
GATE_SLAB = 256
VMEM_LIMIT = 56 * 1024 * 1024

BF16 = jnp.bfloat16
F32 = jnp.float32


def _rms(x, gain):
    return x * lax.rsqrt(jnp.mean(x * x, axis=-1, keepdims=True) + EPS) * gain


def _gelu_exact(x):
    return 0.5 * x * (1.0 + lax.erf(x * math.sqrt(0.5)))


def _dot(a, b):
    return jnp.dot(a, b, preferred_element_type=F32)


def _dot_nt(a, b):
    return lax.dot_general(a, b, (((1,), (1,)), ((), ())), preferred_element_type=F32)


def _mix_in_kernel(x_ref, pos_ref, ln_ref, w_in_ref, pool_w_ref, pool_scale_ref,
                   qn_ref, kn_ref, invf_ref, bd_ref,
                   ypool_ref, q_ref, k_ref, v_ref, ext_ref, *, tile):
    s_idx = pl.program_id(1)
    d_pool = ypool_ref.shape[-1]
    d_attn = q_ref.shape[-1]
    group = d_pool // len(POOL_WINDOWS)

    xn = _rms(x_ref[...], ln_ref[...]).astype(BF16)
    z = _dot(xn, w_in_ref[...])

    @pl.when(s_idx == 0)
    def _():
        ext_ref[0:POOL_HALO, :] = jnp.zeros((POOL_HALO, d_pool), F32)

    @pl.when(s_idx != 0)
    def _():
        ext_ref[0:POOL_HALO, :] = ext_ref[tile:tile + POOL_HALO, :]

    zp = z[:, :d_pool]
    ext_ref[POOL_HALO:POOL_HALO + tile, :] = zp
    t_in_seq = s_idx * tile + lax.broadcasted_iota(jnp.int32, (tile, 1), 0)
    for g, w in enumerate(POOL_WINDOWS):
        lo, hi = g * group, (g + 1) * group
        acc = zp[:, lo:hi]
        for k in range(1, w):
            acc = acc + ext_ref[POOL_HALO - k:POOL_HALO - k + tile, lo:hi]
        cnt = jnp.minimum(t_in_seq + 1, w).astype(F32)
        m = acc / cnt - zp[:, lo:hi]
        y = _dot(m.astype(BF16), pool_w_ref[g])
        ypool_ref[:, lo:hi] = (y * pool_scale_ref[:, lo:hi]).astype(ypool_ref.dtype)

    ang = pos_ref[...].astype(F32) * invf_ref[...]
    cos1 = jnp.cos(ang)
    sin1 = jnp.sin(ang)
    lane = lax.broadcasted_iota(jnp.int32, (1, LANES), 1)
    first_half = (lane % HEAD_DIM) < (HEAD_DIM // 2)
    sin1 = jnp.where(first_half, -sin1, sin1)
    reps = d_attn // LANES
    cos = jnp.concatenate([cos1] * reps, axis=1)
    sin = jnp.concatenate([sin1] * reps, axis=1)
    first_half = jnp.concatenate([first_half] * reps, axis=1)

    def norm_rope(t, gain):
        sq = t * t
        hi_part = sq.astype(BF16)
        lo_part = (sq - hi_part.astype(F32)).astype(BF16)
        ss = _dot(hi_part, bd_ref[...]) + _dot(lo_part, bd_ref[...])
        tn = t * lax.rsqrt(ss * (1.0 / HEAD_DIM) + EPS) * gain
        half = HEAD_DIM // 2
        swapped = jnp.where(first_half,
                            pltpu.roll(tn, d_attn - half, 1),
                            pltpu.roll(tn, half, 1))
        return tn * cos + swapped * sin

    zq = z[:, d_pool:d_pool + d_attn]
    zk = z[:, d_pool + d_attn:d_pool + 2 * d_attn]
    q_ref[...] = (norm_rope(zq, qn_ref[...]) * QK_SCALE).astype(q_ref.dtype)
    k_ref[...] = norm_rope(zk, kn_ref[...]).astype(k_ref.dtype)
    ones_col = jnp.where(lane == 0, 1.0, 0.0).astype(v_ref.dtype) + jnp.zeros((tile, LANES), v_ref.dtype)
    v0 = d_pool + 2 * d_attn
    for hh in range(d_attn // LANES):
        v_ref[:, 2 * hh * LANES:(2 * hh + 1) * LANES] = (
            z[:, v0 + hh * LANES:v0 + (hh + 1) * LANES].astype(v_ref.dtype))
        v_ref[:, (2 * hh + 1) * LANES:(2 * hh + 2) * LANES] = ones_col


def _mix_in(x2, pos2, ln_mix, w_in, pool_w, pool_scale, qn_t, kn_t, invf, bd, *, batch, seq, tile):
    T, D = x2.shape
    d_in = w_in.shape[1]
    d_pool = pool_scale.shape[1]
    d_attn = (d_in - d_pool) // 3
    ns = seq // tile
    row = lambda b, s: (b * ns + s, 0)
    const = lambda b, s: (0, 0)
    return pl.pallas_call(
        functools.partial(_mix_in_kernel, tile=tile),
        grid=(batch, ns),
        in_specs=[
            pl.BlockSpec((tile, D), row),
            pl.BlockSpec((tile, 1), row),
            pl.BlockSpec((1, D), const),
            pl.BlockSpec((D, d_in), const),
            pl.BlockSpec(pool_w.shape, lambda b, s: (0, 0, 0)),
            pl.BlockSpec((1, d_pool), const),
            pl.BlockSpec((1, d_attn), const),
            pl.BlockSpec((1, d_attn), const),
            pl.BlockSpec((1, LANES), const),
            pl.BlockSpec((d_attn, d_attn), const),
        ],
        out_specs=[pl.BlockSpec((tile, d_pool), row), pl.BlockSpec((tile, d_attn), row),
                   pl.BlockSpec((tile, d_attn), row), pl.BlockSpec((tile, 2 * d_attn), row)],
        out_shape=[jax.ShapeDtypeStruct((T, d_pool), BF16), jax.ShapeDtypeStruct((T, d_attn), BF16),
                   jax.ShapeDtypeStruct((T, d_attn), BF16),
                   jax.ShapeDtypeStruct((T, 2 * d_attn), BF16)],
        scratch_shapes=[pltpu.VMEM((POOL_HALO + tile, d_pool), F32)],
        compiler_params=pltpu.CompilerParams(
            dimension_semantics=("arbitrary", "arbitrary"), vmem_limit_bytes=VMEM_LIMIT),
        name="mix_in",
    )(x2, pos2, ln_mix, w_in, pool_w, pool_scale, qn_t, kn_t, invf, bd)


def _diff_attn_kernel(q_ref, k_ref, v_ref, lq1_ref, lk1_ref, lq2_ref, lk2_ref, subln_ref,
                      o_ref, acc_ref, m_ref, *, tq, lam_init, bounded):
    qi = pl.program_id(2)
    hw = 2 * HEAD_DIM
    q = q_ref[...]
    lane = lax.broadcasted_iota(jnp.int32, (1, hw), 1)
    zero = jnp.zeros_like(q)
    qs = jnp.concatenate([jnp.where(lane < HEAD_DIM, q, zero),
                          jnp.where(lane >= HEAD_DIM, q, zero)], axis=0)
    acc_ref[...] = jnp.zeros_like(acc_ref)
    if not bounded:
        m_ref[...] = jnp.full(m_ref.shape, -jnp.inf, F32)

    def step(j, masked):
        off = pl.multiple_of(j * tq, tq)
        s = _dot_nt(qs, k_ref[pl.ds(off, tq), :])
        if masked:
            r = lax.broadcasted_iota(jnp.int32, (tq, tq), 0)
            c = lax.broadcasted_iota(jnp.int32, (tq, tq), 1)
            keep = jnp.concatenate([c <= r, c <= r], axis=0)
            s = jnp.where(keep, s, -jnp.inf)
        v_t = v_ref[pl.ds(off, tq), :]
        if bounded:
            acc_ref[...] += _dot(jnp.exp2(s).astype(BF16), v_t)
        else:
            m_prev = m_ref[...]
            m_new = jnp.maximum(m_prev, jnp.max(s, axis=1, keepdims=True))
            m_ref[...] = m_new
            acc_ref[...] = (jnp.exp2(m_prev - m_new) * acc_ref[...]
                            + _dot(jnp.exp2(s - m_new).astype(BF16), v_t))

    def body(j, carry):
        step(j, False)
        return carry

    lax.fori_loop(0, qi, body, 0)
    step(qi, True)

    lam = (jnp.exp(jnp.sum(lq1_ref[...] * lk1_ref[...], axis=1, keepdims=True))
           - jnp.exp(jnp.sum(lq2_ref[...] * lk2_ref[...], axis=1, keepdims=True)) + lam_init)
    acc = acc_ref[...]
    num, den = acc[:, :hw], acc[:, hw:hw + 1]
    o = num[:tq] / den[:tq] - lam * (num[tq:] / den[tq:])
    o_ref[...] = (_rms(o, subln_ref[...]) * (1.0 - lam_init)).astype(o_ref.dtype)


def _diff_attn(q, k, v, lq1, lk1, lq2, lk2, subln, *, batch, seq, tq, lam_init, bounded):
    T, d_attn = q.shape
    heads = d_attn // (2 * HEAD_DIM)
    nq = seq // tq
    hw = 2 * HEAD_DIM
    vec = pl.BlockSpec((1, HEAD_DIM), lambda b, h, i: (0, 0))
    return pl.pallas_call(
        functools.partial(_diff_attn_kernel, tq=tq, lam_init=lam_init, bounded=bounded),
        grid=(batch, heads, nq),
        in_specs=[
            pl.BlockSpec((tq, hw), lambda b, h, i: (b * nq + i, h)),
            pl.BlockSpec((seq, hw), lambda b, h, i: (b, h)),
            pl.BlockSpec((seq, 2 * hw), lambda b, h, i: (b, h)),
            vec, vec, vec, vec,
            pl.BlockSpec((1, hw), lambda b, h, i: (0, 0)),
        ],
        out_specs=pl.BlockSpec((tq, hw), lambda b, h, i: (b * nq + i, h)),
        out_shape=jax.ShapeDtypeStruct((T, d_attn), BF16),
        scratch_shapes=[pltpu.VMEM((2 * tq, 2 * hw), F32), pltpu.VMEM((2 * tq, 1), F32)],
        compiler_params=pltpu.CompilerParams(
            dimension_semantics=("arbitrary", "arbitrary", "arbitrary"),
            vmem_limit_bytes=VMEM_LIMIT),
        name="diff_attn_bounded" if bounded else "diff_attn_online",
    )(q, k, v, lq1, lk1, lq2, lk2, subln)


def _peer_front_kernel(x_ref, ypool_ref, o_ref, wo_ref, ln_ref, wq_ref, sk_ref,
                       h1_ref, xn_ref, rank2_ref, eb_ref, quota_ref, ea_ref,
                       qp_ref, a_ref, b_ref, stage_ref):
    d_pool = ypool_ref.shape[-1]
    tile = x_ref.shape[0]
    h1 = (x_ref[...] + _dot(ypool_ref[...], wo_ref[0:d_pool, :])
          + _dot(o_ref[...], wo_ref[d_pool:, :]))
    h1_ref[...] = h1
    xn = _rms(h1, ln_ref[...]).astype(BF16)
    xn_ref[...] = xn
    qp_ref[...] = _dot(xn, wq_ref[...]).astype(BF16)

    neg = jnp.float32(-jnp.inf)

    def extract_sorted(x, dst_ref, want_rank):
        rank = jnp.full(x.shape, float(PEER_TOPK), F32) if want_rank else None
        for r in range(PEER_TOPK):
            m = jnp.max(x, axis=0, keepdims=True)
            dst_ref[r:r + 1, :] = m
            hit = x == m
            if want_rank:
                rank = jnp.where(hit, float(r), rank)
            x = jnp.where(hit, neg, x)
        return rank

    def chunk(c, carry):
        qp_c = qp_ref[pl.ds(pl.multiple_of(c * LANES, LANES), LANES), :]
        for h in range(N_PEER_HEADS):
            c0 = (2 * h) * N_KEYS
            s1 = _dot_nt(sk_ref[h, 0], qp_c[:, c0:c0 + N_KEYS])
            s2 = _dot_nt(sk_ref[h, 1], qp_c[:, c0 + N_KEYS:c0 + 2 * N_KEYS])
            extract_sorted(s1, a_ref, False)
            rank2 = extract_sorted(s2, b_ref, True)
            a = a_ref[...]
            b = b_ref[...]
            cand = jnp.concatenate(
                [a + b[0:1, :]]
                + [a[0:ROWS_F32, :] + b[q:q + 1, :] for q in range(1, ROWS_F32)]
                + [a[0:1, :] + b[ROWS_F32:, :]], axis=0)
            c_left = cand
            tau = None
            for r in range(PEER_TOPK):
                tau = jnp.max(c_left, axis=0, keepdims=True)
                c_left = jnp.where(c_left == tau, neg, c_left)
            top = a[0:1, :] + b[0:1, :]
            zsum = jnp.sum(jnp.where(cand >= tau, jnp.exp(cand - top), 0.0), axis=0, keepdims=True)
            quota_sorted = jnp.zeros(a.shape, F32)
            for q in range(PEER_TOPK):
                quota_sorted = quota_sorted + jnp.where(a + b[q:q + 1, :] >= tau, 1.0, 0.0)
            quota = jnp.zeros(s1.shape, F32)
            for p in range(PEER_TOPK):
                quota = jnp.where(s1 == a[p:p + 1, :], quota_sorted[p:p + 1, :], quota)
            stage_ref[0, c, h] = rank2
            stage_ref[1, c, h] = jnp.exp(s2 - b[0:1, :])
            stage_ref[2, c, h] = quota
            stage_ref[3, c, h] = jnp.exp(s1 - a[0:1, :]) / zsum
        return carry

    lax.fori_loop(0, tile // LANES, chunk, 0)
    per_slab = GATE_SLAB // LANES
    for k, dst in enumerate((rank2_ref, eb_ref, quota_ref, ea_ref)):
        for c in range(tile // LANES):
            lanes = slice((c % per_slab) * LANES, (c % per_slab + 1) * LANES)
            dst[c // per_slab, :, :, lanes] = stage_ref[k, c].astype(dst.dtype)


def _peer_front(x2, ypool, o, w_o, ln_ffn, w_q, subkeys, *, tile):
    T, D = x2.shape
    d_pool = ypool.shape[1]
    nh = subkeys.shape[0]
    row = lambda i: (i, 0)
    const = lambda i: (0, 0)
    tok = pl.BlockSpec((tile // GATE_SLAB, nh, N_KEYS, GATE_SLAB), lambda i: (i, 0, 0, 0))
    gate_shape = (T // GATE_SLAB, nh, N_KEYS, GATE_SLAB)
    return pl.pallas_call(
        _peer_front_kernel,
        grid=(T // tile,),
        in_specs=[
            pl.BlockSpec((tile, D), row),
            pl.BlockSpec((tile, d_pool), row),
            pl.BlockSpec((tile, o.shape[1]), row),
            pl.BlockSpec(w_o.shape, const),
            pl.BlockSpec((1, D), const),
            pl.BlockSpec(w_q.shape, const),
            pl.BlockSpec(subkeys.shape, lambda i: (0, 0, 0, 0)),
        ],
        out_specs=[pl.BlockSpec((tile, D), row), pl.BlockSpec((tile, D), row), tok, tok, tok, tok],
        out_shape=[
            jax.ShapeDtypeStruct((T, D), F32),
            jax.ShapeDtypeStruct((T, D), BF16),
            jax.ShapeDtypeStruct(gate_shape, BF16),
            jax.ShapeDtypeStruct(gate_shape, BF16),
            jax.ShapeDtypeStruct(gate_shape, F32),
            jax.ShapeDtypeStruct(gate_shape, F32),
        ],
        scratch_shapes=[
            pltpu.VMEM((tile, w_q.shape[1]), BF16),
            pltpu.VMEM((PEER_TOPK, LANES), F32),
            pltpu.VMEM((PEER_TOPK, LANES), F32),
            pltpu.VMEM((4, tile // LANES, nh, N_KEYS, LANES), F32),
        ],
        compiler_params=pltpu.CompilerParams(
            dimension_semantics=("arbitrary",), vmem_limit_bytes=VMEM_LIMIT),
        name="peer_front",
    )(x2, ypool, o, w_o, ln_ffn, w_q, subkeys)


def _peer_dense_kernel(xn_ref, h1_ref, u_ref, vt_ref, rank2_ref, eb_ref, quota_ref, ea_ref,
                       out_ref, acc_ref, hid_ref, gate_ref, *, keys_per_block):
    e_idx = pl.program_id(1)
    n_blocks = pl.num_programs(1) - 1
    tile = xn_ref.shape[0]
    n_row_chunks = N_KEYS // ROWS_BF16
    slot = e_idx % 2

    @pl.when(e_idx == 0)
    def _():
        acc_ref[...] = jnp.zeros_like(acc_ref)
        hid_ref[1] = jnp.zeros(hid_ref.shape[1:], hid_ref.dtype)

    @pl.when(e_idx < n_blocks)
    def _():
        for ii in range(keys_per_block):
            i = e_idx * keys_per_block + ii
            for tc in range(tile // GATE_SLAB):
                lanes = slice(tc * GATE_SLAB, (tc + 1) * GATE_SLAB)
                gates = [jnp.zeros((ROWS_BF16, GATE_SLAB), BF16)] * n_row_chunks
                for h in range(N_PEER_HEADS):
                    quota = jnp.broadcast_to(quota_ref[tc, h, pl.ds(i, 1), :],
                                             (ROWS_BF16, GATE_SLAB)).astype(BF16)
                    ea = jnp.broadcast_to(ea_ref[tc, h, pl.ds(i, 1), :],
                                          (ROWS_BF16, GATE_SLAB)).astype(BF16)
                    for c in range(n_row_chunks):
                        rows = slice(c * ROWS_BF16, (c + 1) * ROWS_BF16)
                        w = eb_ref[tc, h, rows, :] * ea
                        gates[c] = gates[c] + jnp.where(rank2_ref[tc, h, rows, :] < quota, w,
                                                        jnp.zeros_like(w))
                for c in range(n_row_chunks):
                    r0 = ii * N_KEYS + c * ROWS_BF16
                    gate_ref[r0:r0 + ROWS_BF16, lanes] = gates[c]
        a_t = _dot_nt(u_ref[...], xn_ref[...])
        acc_ref[...] += _dot(vt_ref[...], hid_ref[1 - slot])
        hid_ref[slot] = _gelu_exact(a_t).astype(BF16) * gate_ref[...]

    @pl.when(e_idx == n_blocks)
    def _():
        acc = acc_ref[...] + _dot(vt_ref[...], hid_ref[1 - slot])
        out_ref[...] = h1_ref[...] + acc.T


def _peer_dense(xn, h1, u, vt_blocks, rank2, eb, quota, ea, *, tile):
    T, D = xn.shape
    n_blocks, _, eblk = vt_blocks.shape
    nh = rank2.shape[1]
    row = lambda t, e: (t, 0)
    tok = pl.BlockSpec((tile // GATE_SLAB, nh, N_KEYS, GATE_SLAB), lambda t, e: (t, 0, 0, 0))
    return pl.pallas_call(
        functools.partial(_peer_dense_kernel, keys_per_block=eblk // N_KEYS),
        grid=(T // tile, n_blocks + 1),
        in_specs=[
            pl.BlockSpec((tile, D), row),
            pl.BlockSpec((tile, D), row),
            pl.BlockSpec((eblk, D), lambda t, e: (jnp.minimum(e, n_blocks - 1), 0)),
            pl.BlockSpec((None, D, eblk), lambda t, e: (jnp.maximum(e - 1, 0), 0, 0)),
            tok, tok, tok, tok,
        ],
        out_specs=pl.BlockSpec((tile, D), row),
        out_shape=jax.ShapeDtypeStruct((T, D), F32),
        scratch_shapes=[pltpu.VMEM((D, tile), F32), pltpu.VMEM((2, eblk, tile), BF16),
                        pltpu.VMEM((eblk, tile), BF16)],
        compiler_params=pltpu.CompilerParams(
            dimension_semantics=("arbitrary", "arbitrary"), vmem_limit_bytes=VMEM_LIMIT),
        name="peer_dense",
    )(xn, h1, u, vt_blocks, rank2, eb, quota, ea)


def _ple_kernel(h_ref, p_ref, ln_ref, wg_ref, wp_ref, out_ref):
    h = h_ref[...]
    gate = jax.nn.sigmoid(_dot(_rms(h, ln_ref[...]).astype(BF16), wg_ref[...]))
    out_ref[...] = h + gate * _dot(p_ref[...].astype(BF16), wp_ref[...])


def _ple(h2, p2, ln_pe, w_gate, w_proj, *, tile):
    T, D = h2.shape
    row = lambda i: (i, 0)
    const = lambda i: (0, 0)
    return pl.pallas_call(
        _ple_kernel,
        grid=(T // tile,),
        in_specs=[
            pl.BlockSpec((tile, D), row),
            pl.BlockSpec((tile, p2.shape[1]), row),
            pl.BlockSpec((1, D), const),
            pl.BlockSpec(w_gate.shape, const),
            pl.BlockSpec(w_proj.shape, const),
        ],
        out_specs=pl.BlockSpec((tile, D), row),
        out_shape=jax.ShapeDtypeStruct((T, D), F32),
        compiler_params=pltpu.CompilerParams(
            dimension_semantics=("arbitrary",), vmem_limit_bytes=VMEM_LIMIT),
        name="ple",
    )(h2, p2, ln_pe, w_gate, w_proj)


PEER_EXPERT_BLOCK = 512


def _pick_tile(n, pref):
    t = min(pref, n)
    assert n % t == 0, (n, t)
    return t


def kernel(x, p, positions, ln_mix, w_in, pool_w, pool_scale, q_norm, k_norm,
           lambda_q1, lambda_k1, lambda_q2, lambda_k2, subln, w_o, ln_ffn,
           w_peer_q, peer_subkeys, peer_u, peer_v, ln_pe, w_pe_gate, w_pe_proj):
    B, S, D = x.shape
    T = B * S
    depth = w_in.shape[0]
    d_pool = pool_scale.shape[-1]
    d_attn = (w_in.shape[-1] - d_pool) // 3
    n_groups = d_attn // HEAD_DIM
    n_exp = peer_u.shape[1]

    lane = jnp.arange(LANES)
    invf = (ROPE_THETA ** (-(2.0 * (lane % (HEAD_DIM // 2))).astype(F32) / HEAD_DIM)).reshape(1, LANES)
    gid = jnp.arange(d_attn) // HEAD_DIM
    bd = (gid[:, None] == gid[None, :]).astype(BF16)

    pos2 = positions.reshape(T, 1)
    h = x.reshape(T, D)
    for i in range(depth):
        lam_init = 0.8 - 0.6 * math.exp(-0.3 * i)
        ypool, q, k, v = _mix_in(
            h, pos2, ln_mix[i][None], w_in[i].astype(BF16), pool_w[i].astype(BF16),
            pool_scale[i][None], jnp.tile(q_norm[i], n_groups)[None],
            jnp.tile(k_norm[i], n_groups)[None], invf, bd,
            batch=B, seq=S, tile=_pick_tile(S, 512))
        attn = functools.partial(_diff_attn, batch=B, seq=S, tq=_pick_tile(S, 512),
                                 lam_init=lam_init)
        score_bound = (HEAD_DIM * QK_SCALE * 1.02
                       * jnp.max(jnp.abs(q_norm[i])) * jnp.max(jnp.abs(k_norm[i])))
        o = lax.cond(score_bound <= EXP2_SAFE_RANGE,
                     functools.partial(attn, bounded=True), functools.partial(attn, bounded=False),
                     q, k, v, lambda_q1[i][None], lambda_k1[i][None], lambda_q2[i][None],
                     lambda_k2[i][None], subln[i][None])
        h1, xn, rank2, eb, quota, ea = _peer_front(
            h, ypool, o, w_o[i].astype(BF16), ln_ffn[i][None], w_peer_q[i].astype(BF16),
            peer_subkeys[i].astype(BF16), tile=_pick_tile(T, 256))
        vt_blocks = (peer_v[i].astype(BF16).reshape(n_exp // PEER_EXPERT_BLOCK, PEER_EXPERT_BLOCK, D)
                     .transpose(0, 2, 1))
        h2 = _peer_dense(xn, h1, peer_u[i].astype(BF16), vt_blocks, rank2, eb, quota, ea,
                         tile=_pick_tile(T, 512))
        h = _ple(h2, p[i].reshape(T, -1), ln_pe[i][None], w_pe_gate[i].astype(BF16),
                 w_pe_proj[i].astype(BF16), tile=_pick_tile(T, 512))
    return h.reshape(B, S, D)
```

```python
import functools
import math

import jax
import jax.numpy as jnp
from jax import lax
from jax.experimental import pallas as pl
from jax.experimental.pallas import tpu as pltpu

EPS = 1e-6
ROPE_THETA = 10000.0
POOL_WINDOWS = (2, 4, 8, 16)
POOL_HALO = 16
HEAD_DIM = 64
QK_SCALE = math.log2(math.e) / math.sqrt(HEAD_DIM)
EXP2_SAFE_RANGE = 60.0
N_KEYS = 128
PEER_TOPK = 16
N_PEER_HEADS = 8
LANES = 128
ROWS_F32 = 8
ROWS_BF16 = 16
GATE_SLAB = 256
VMEM_LIMIT = 56 * 1024 * 1024

BF16 = jnp.bfloat16
F32 = jnp.float32


def _rms(x, gain):
    return x * lax.rsqrt(jnp.mean(x * x, axis=-1, keepdims=True) + EPS) * gain


def _gelu_exact(x):
    return 0.5 * x * (1.0 + lax.erf(x * math.sqrt(0.5)))


def _dot(a, b):
    return jnp.dot(a, b, preferred_element_type=F32)


def _dot_nt(a, b):
    return lax.dot_general(a, b, (((1,), (1,)), ((), ())), preferred_element_type=F32)


def _mix_in_kernel(x_ref, pos_ref, ln_ref, w_in_ref, pool_w_ref, pool_scale_ref,
                   qn_ref, kn_ref, invf_ref, bd_ref,
                   ypool_ref, q_ref, k_ref, v_ref, ext_ref, *, tile):
    s_idx = pl.program_id(1)
    d_pool = ypool_ref.shape[-1]
    d_attn = q_ref.shape[-1]
    group = d_pool // len(POOL_WINDOWS)

    xn = _rms(x_ref[...], ln_ref[...]).astype(BF16)
    z = _dot(xn, w_in_ref[...])

    @pl.when(s_idx == 0)
    def _():
        ext_ref[0:POOL_HALO, :] = jnp.zeros((POOL_HALO, d_pool), F32)

    @pl.when(s_idx != 0)
    def _():
        ext_ref[0:POOL_HALO, :] = ext_ref[tile:tile + POOL_HALO, :]

    zp = z[:, :d_pool]
    ext_ref[POOL_HALO:POOL_HALO + tile, :] = zp
    t_in_seq = s_idx * tile + lax.broadcasted_iota(jnp.int32, (tile, 1), 0)
    for g, w in enumerate(POOL_WINDOWS):
        lo, hi = g * group, (g + 1) * group
        acc = zp[:, lo:hi]
        for k in range(1, w):
            acc = acc + ext_ref[POOL_HALO - k:POOL_HALO - k + tile, lo:hi]
        cnt = jnp.minimum(t_in_seq + 1, w).astype(F32)
        m = acc / cnt - zp[:, lo:hi]
        y = _dot(m.astype(BF16), pool_w_ref[g])
        ypool_ref[:, lo:hi] = (y * pool_scale_ref[:, lo:hi]).astype(ypool_ref.dtype)

    ang = pos_ref[...].astype(F32) * invf_ref[...]
    cos1 = jnp.cos(ang)
    sin1 = jnp.sin(ang)
    lane = lax.broadcasted_iota(jnp.int32, (1, LANES), 1)
    first_half = (lane % HEAD_DIM) < (HEAD_DIM // 2)
    sin1 = jnp.where(first_half, -sin1, sin1)
    reps = d_attn // LANES
    cos = jnp.concatenate([cos1] * reps, axis=1)
    sin = jnp.concatenate([sin1] * reps, axis=1)
    first_half = jnp.concatenate([first_half] * reps, axis=1)

    def norm_rope(t, gain):
        sq = t * t
        hi_part = sq.astype(BF16)
        lo_part = (sq - hi_part.astype(F32)).astype(BF16)
        ss = _dot(hi_part, bd_ref[...]) + _dot(lo_part, bd_ref[...])
        tn = t * lax.rsqrt(ss * (1.0 / HEAD_DIM) + EPS) * gain
        half = HEAD_DIM // 2
        swapped = jnp.where(first_half,
                            pltpu.roll(tn, d_attn - half, 1),
                            pltpu.roll(tn, half, 1))
        return tn * cos + swapped * sin

    zq = z[:, d_pool:d_pool + d_attn]
    zk = z[:, d_pool + d_attn:d_pool + 2 * d_attn]
    q_ref[...] = (norm_rope(zq, qn_ref[...]) * QK_SCALE).astype(q_ref.dtype)
    k_ref[...] = norm_rope(zk, kn_ref[...]).astype(k_ref.dtype)
    ones_col = jnp.where(lane == 0, 1.0, 0.0).astype(v_ref.dtype) + jnp.zeros((tile, LANES), v_ref.dtype)
    v0 = d_pool + 2 * d_attn
    for hh in range(d_attn // LANES):
        v_ref[:, 2 * hh * LANES:(2 * hh + 1) * LANES] = (
            z[:, v0 + hh * LANES:v0 + (hh + 1) * LANES].astype(v_ref.dtype))
        v_ref[:, (2 * hh + 1) * LANES:(2 * hh + 2) * LANES] = ones_col


def _mix_in(x2, pos2, ln_mix, w_in, pool_w, pool_scale, qn_t, kn_t, invf, bd, *, batch, seq, tile):
    T, D = x2.shape
    d_in = w_in.shape[1]
    d_pool = pool_scale.shape[1]
    d_attn = (d_in - d_pool) // 3
    ns = seq // tile
    row = lambda b, s: (b * ns + s, 0)
    const = lambda b, s: (0, 0)
    return pl.pallas_call(
        functools.partial(_mix_in_kernel, tile=tile),
        grid=(batch, ns),
        in_specs=[
            pl.BlockSpec((tile, D), row),
            pl.BlockSpec((tile, 1), row),
            pl.BlockSpec((1, D), const),
            pl.BlockSpec((D, d_in), const),
            pl.BlockSpec(pool_w.shape, lambda b, s: (0, 0, 0)),
            pl.BlockSpec((1, d_pool), const),
            pl.BlockSpec((1, d_attn), const),
            pl.BlockSpec((1, d_attn), const),
            pl.BlockSpec((1, LANES), const),
            pl.BlockSpec((d_attn, d_attn), const),
        ],
        out_specs=[pl.BlockSpec((tile, d_pool), row), pl.BlockSpec((tile, d_attn), row),
                   pl.BlockSpec((tile, d_attn), row), pl.BlockSpec((tile, 2 * d_attn), row)],
        out_shape=[jax.ShapeDtypeStruct((T, d_pool), BF16), jax.ShapeDtypeStruct((T, d_attn), BF16),
                   jax.ShapeDtypeStruct((T, d_attn), BF16),
                   jax.ShapeDtypeStruct((T, 2 * d_attn), BF16)],
        scratch_shapes=[pltpu.VMEM((POOL_HALO + tile, d_pool), F32)],
        compiler_params=pltpu.CompilerParams(
            dimension_semantics=("arbitrary", "arbitrary"), vmem_limit_bytes=VMEM_LIMIT),
        name="mix_in",
    )(x2, pos2, ln_mix, w_in, pool_w, pool_scale, qn_t, kn_t, invf, bd)


def _diff_attn_kernel(q_ref, k_ref, v_ref, lq1_ref, lk1_ref, lq2_ref, lk2_ref, subln_ref,
                      o_ref, acc_ref, m_ref, *, tq, lam_init, bounded):
    qi = pl.program_id(2)
    hw = 2 * HEAD_DIM
    q = q_ref[...]
    lane = lax.broadcasted_iota(jnp.int32, (1, hw), 1)
    zero = jnp.zeros_like(q)
    qs = jnp.concatenate([jnp.where(lane < HEAD_DIM, q, zero),
                          jnp.where(lane >= HEAD_DIM, q, zero)], axis=0)
    acc_ref[...] = jnp.zeros_like(acc_ref)
    if not bounded:
        m_ref[...] = jnp.full(m_ref.shape, -jnp.inf, F32)

    def step(j, masked):
        off = pl.multiple_of(j * tq, tq)
        s = _dot_nt(qs, k_ref[pl.ds(off, tq), :])
        if masked:
            r = lax.broadcasted_iota(jnp.int32, (tq, tq), 0)
            c = lax.broadcasted_iota(jnp.int32, (tq, tq), 1)
            keep = jnp.concatenate([c <= r, c <= r], axis=0)
            s = jnp.where(keep, s, -jnp.inf)
        v_t = v_ref[pl.ds(off, tq), :]
        if bounded:
            acc_ref[...] += _dot(jnp.exp2(s).astype(BF16), v_t)
        else:
            m_prev = m_ref[...]
            m_new = jnp.maximum(m_prev, jnp.max(s, axis=1, keepdims=True))
            m_ref[...] = m_new
            acc_ref[...] = (jnp.exp2(m_prev - m_new) * acc_ref[...]
                            + _dot(jnp.exp2(s - m_new).astype(BF16), v_t))

    def body(j, carry):
        step(j, False)
        return carry

    lax.fori_loop(0, qi, body, 0)
    step(qi, True)

    lam = (jnp.exp(jnp.sum(lq1_ref[...] * lk1_ref[...], axis=1, keepdims=True))
           - jnp.exp(jnp.sum(lq2_ref[...] * lk2_ref[...], axis=1, keepdims=True)) + lam_init)
    acc = acc_ref[...]
    num, den = acc[:, :hw], acc[:, hw:hw + 1]
    o = num[:tq] / den[:tq] - lam * (num[tq:] / den[tq:])
    o_ref[...] = (_rms(o, subln_ref[...]) * (1.0 - lam_init)).astype(o_ref.dtype)


def _diff_attn(q, k, v, lq1, lk1, lq2, lk2, subln, *, batch, seq, tq, lam_init, bounded):
    T, d_attn = q.shape
    heads = d_attn // (2 * HEAD_DIM)
    nq = seq // tq
    hw = 2 * HEAD_DIM
    vec = pl.BlockSpec((1, HEAD_DIM), lambda b, h, i: (0, 0))
    return pl.pallas_call(
        functools.partial(_diff_attn_kernel, tq=tq, lam_init=lam_init, bounded=bounded),
        grid=(batch, heads, nq),
        in_specs=[
            pl.BlockSpec((tq, hw), lambda b, h, i: (b * nq + i, h)),
            pl.BlockSpec((seq, hw), lambda b, h, i: (b, h)),
            pl.BlockSpec((seq, 2 * hw), lambda b, h, i: (b, h)),
            vec, vec, vec, vec,
            pl.BlockSpec((1, hw), lambda b, h, i: (0, 0)),
        ],
        out_specs=pl.BlockSpec((tq, hw), lambda b, h, i: (b * nq + i, h)),
        out_shape=jax.ShapeDtypeStruct((T, d_attn), BF16),
        scratch_shapes=[pltpu.VMEM((2 * tq, 2 * hw), F32), pltpu.VMEM((2 * tq, 1), F32)],
        compiler_params=pltpu.CompilerParams(
            dimension_semantics=("arbitrary", "arbitrary", "arbitrary"),
            vmem_limit_bytes=VMEM_LIMIT),
        name="diff_attn_bounded" if bounded else "diff_attn_online",
    )(q, k, v, lq1, lk1, lq2, lk2, subln)


def _peer_front_kernel(x_ref, ypool_ref, o_ref, wo_ref, ln_ref, wq_ref, sk_ref,
                       h1_ref, xn_ref, rank2_ref, eb_ref, quota_ref, ea_ref,
                       qp_ref, a_ref, b_ref, stage_ref):
    d_pool = ypool_ref.shape[-1]
    tile = x_ref.shape[0]
    h1 = (x_ref[...] + _dot(ypool_ref[...], wo_ref[0:d_pool, :])
          + _dot(o_ref[...], wo_ref[d_pool:, :]))
    h1_ref[...] = h1
    xn = _rms(h1, ln_ref[...]).astype(BF16)
    xn_ref[...] = xn
    qp_ref[...] = _dot(xn, wq_ref[...]).astype(BF16)

    neg = jnp.float32(-jnp.inf)

    def extract_sorted(x, dst_ref, want_rank):
        rank = jnp.full(x.shape, float(PEER_TOPK), F32) if want_rank else None
        for r in range(PEER_TOPK):
            m = jnp.max(x, axis=0, keepdims=True)
            dst_ref[r:r + 1, :] = m
            hit = x == m
            if want_rank:
                rank = jnp.where(hit, float(r), rank)
            x = jnp.where(hit, neg, x)
        return rank

    def chunk(c, carry):
        qp_c = qp_ref[pl.ds(pl.multiple_of(c * LANES, LANES), LANES), :]
        for h in range(N_PEER_HEADS):
            c0 = (2 * h) * N_KEYS
            s1 = _dot_nt(sk_ref[h, 0], qp_c[:, c0:c0 + N_KEYS])
            s2 = _dot_nt(sk_ref[h, 1], qp_c[:, c0 + N_KEYS:c0 + 2 * N_KEYS])
            extract_sorted(s1, a_ref, False)
            rank2 = extract_sorted(s2, b_ref, True)
            a = a_ref[...]
            b = b_ref[...]
            cand = jnp.concatenate(
                [a + b[0:1, :]]
                + [a[0:ROWS_F32, :] + b[q:q + 1, :] for q in range(1, ROWS_F32)]
                + [a[0:1, :] + b[ROWS_F32:, :]], axis=0)
            c_left = cand
            tau = None
            for r in range(PEER_TOPK):
                tau = jnp.max(c_left, axis=0, keepdims=True)
                c_left = jnp.where(c_left == tau, neg, c_left)
            top = a[0:1, :] + b[0:1, :]
            zsum = jnp.sum(jnp.where(cand >= tau, jnp.exp(cand - top), 0.0), axis=0, keepdims=True)
            quota_sorted = jnp.zeros(a.shape, F32)
            for q in range(PEER_TOPK):
                quota_sorted = quota_sorted + jnp.where(a + b[q:q + 1, :] >= tau, 1.0, 0.0)
            quota = jnp.zeros(s1.shape, F32)
            for p in range(PEER_TOPK):
                quota = jnp.where(s1 == a[p:p + 1, :], quota_sorted[p:p + 1, :], quota)
            stage_ref[0, c, h] = rank2
            stage_ref[1, c, h] = jnp.exp(s2 - b[0:1, :])
            stage_ref[2, c, h] = quota
            stage_ref[3, c, h] = jnp.exp(s1 - a[0:1, :]) / zsum
        return carry

    lax.fori_loop(0, tile // LANES, chunk, 0)
    per_slab = GATE_SLAB // LANES
    for k, dst in enumerate((rank2_ref, eb_ref, quota_ref, ea_ref)):
        for c in range(tile // LANES):
            lanes = slice((c % per_slab) * LANES, (c % per_slab + 1) * LANES)
            dst[c // per_slab, :, :, lanes] = stage_ref[k, c].astype(dst.dtype)


def _peer_front(x2, ypool, o, w_o, ln_ffn, w_q, subkeys, *, tile):
    T, D = x2.shape
    d_pool = ypool.shape[1]
    nh = subkeys.shape[0]
    row = lambda i: (i, 0)
    const = lambda i: (0, 0)
    tok = pl.BlockSpec((tile // GATE_SLAB, nh, N_KEYS, GATE_SLAB), lambda i: (i, 0, 0, 0))
    gate_shape = (T // GATE_SLAB, nh, N_KEYS, GATE_SLAB)
    return pl.pallas_call(
        _peer_front_kernel,
        grid=(T // tile,),
        in_specs=[
            pl.BlockSpec((tile, D), row),
            pl.BlockSpec((tile, d_pool), row),
            pl.BlockSpec((tile, o.shape[1]), row),
            pl.BlockSpec(w_o.shape, const),
            pl.BlockSpec((1, D), const),
            pl.BlockSpec(w_q.shape, const),
            pl.BlockSpec(subkeys.shape, lambda i: (0, 0, 0, 0)),
        ],
        out_specs=[pl.BlockSpec((tile, D), row), pl.BlockSpec((tile, D), row), tok, tok, tok, tok],
        out_shape=[
            jax.ShapeDtypeStruct((T, D), F32),
            jax.ShapeDtypeStruct((T, D), BF16),
            jax.ShapeDtypeStruct(gate_shape, BF16),
            jax.ShapeDtypeStruct(gate_shape, BF16),
            jax.ShapeDtypeStruct(gate_shape, F32),
            jax.ShapeDtypeStruct(gate_shape, F32),
        ],
        scratch_shapes=[
            pltpu.VMEM((tile, w_q.shape[1]), BF16),
            pltpu.VMEM((PEER_TOPK, LANES), F32),
            pltpu.VMEM((PEER_TOPK, LANES), F32),
            pltpu.VMEM((4, tile // LANES, nh, N_KEYS, LANES), F32),
        ],
        compiler_params=pltpu.CompilerParams(
            dimension_semantics=("arbitrary",), vmem_limit_bytes=VMEM_LIMIT),
        name="peer_front",
    )(x2, ypool, o, w_o, ln_ffn, w_q, subkeys)


def _peer_dense_kernel(xn_ref, h1_ref, u_ref, vt_ref, rank2_ref, eb_ref, quota_ref, ea_ref,
                       out_ref, acc_ref, hid_ref, gate_ref, *, keys_per_block):
    e_idx = pl.program_id(1)
    n_blocks = pl.num_programs(1) - 1
    tile = xn_ref.shape[0]
    n_row_chunks = N_KEYS // ROWS_BF16
    slot = e_idx % 2

    @pl.when(e_idx == 0)
    def _():
        acc_ref[...] = jnp.zeros_like(acc_ref)
        hid_ref[1] = jnp.zeros(hid_ref.shape[1:], hid_ref.dtype)

    @pl.when(e_idx < n_blocks)
    def _():
        for ii in range(keys_per_block):
            i = e_idx * keys_per_block + ii
            for tc in range(tile // GATE_SLAB):
                lanes = slice(tc * GATE_SLAB, (tc + 1) * GATE_SLAB)
                gates = [jnp.zeros((ROWS_BF16, GATE_SLAB), BF16)] * n_row_chunks
                for h in range(N_PEER_HEADS):
                    quota = jnp.broadcast_to(quota_ref[tc, h, pl.ds(i, 1), :],
                                             (ROWS_BF16, GATE_SLAB)).astype(BF16)
                    ea = jnp.broadcast_to(ea_ref[tc, h, pl.ds(i, 1), :],
                                          (ROWS_BF16, GATE_SLAB)).astype(BF16)
                    for c in range(n_row_chunks):
                        rows = slice(c * ROWS_BF16, (c + 1) * ROWS_BF16)
                        w = eb_ref[tc, h, rows, :] * ea
                        gates[c] = gates[c] + jnp.where(rank2_ref[tc, h, rows, :] < quota, w,
                                                        jnp.zeros_like(w))
                for c in range(n_row_chunks):
                    r0 = ii * N_KEYS + c * ROWS_BF16
                    gate_ref[r0:r0 + ROWS_BF16, lanes] = gates[c]
        a_t = _dot_nt(u_ref[...], xn_ref[...])
        acc_ref[...] += _dot(vt_ref[...], hid_ref[1 - slot])
        hid_ref[slot] = _gelu_exact(a_t).astype(BF16) * gate_ref[...]

    @pl.when(e_idx == n_blocks)
    def _():
        acc = acc_ref[...] + _dot(vt_ref[...], hid_ref[1 - slot])
        out_ref[...] = h1_ref[...] + acc.T


def _peer_dense(xn, h1, u, vt_blocks, rank2, eb, quota, ea, *, tile):
    T, D = xn.shape
    n_blocks, _, eblk = vt_blocks.shape
    nh = rank2.shape[1]
    row = lambda t, e: (t, 0)
    tok = pl.BlockSpec((tile // GATE_SLAB, nh, N_KEYS, GATE_SLAB), lambda t, e: (t, 0, 0, 0))
    return pl.pallas_call(
        functools.partial(_peer_dense_kernel, keys_per_block=eblk // N_KEYS),
        grid=(T // tile, n_blocks + 1),
        in_specs=[
            pl.BlockSpec((tile, D), row),
            pl.BlockSpec((tile, D), row),
            pl.BlockSpec((eblk, D), lambda t, e: (jnp.minimum(e, n_blocks - 1), 0)),
            pl.BlockSpec((None, D, eblk), lambda t, e: (jnp.maximum(e - 1, 0), 0, 0)),
            tok, tok, tok, tok,
        ],
        out_specs=pl.BlockSpec((tile, D), row),
        out_shape=jax.ShapeDtypeStruct((T, D), F32),
        scratch_shapes=[pltpu.VMEM((D, tile), F32), pltpu.VMEM((2, eblk, tile), BF16),
                        pltpu.VMEM((eblk, tile), BF16)],
        compiler_params=pltpu.CompilerParams(
            dimension_semantics=("arbitrary", "arbitrary"), vmem_limit_bytes=VMEM_LIMIT),
        name="peer_dense",
    )(xn, h1, u, vt_blocks, rank2, eb, quota, ea)


def _ple_kernel(h_ref, p_ref, ln_ref, wg_ref, wp_ref, out_ref):
    h = h_ref[...]
    gate = jax.nn.sigmoid(_dot(_rms(h, ln_ref[...]).astype(BF16), wg_ref[...]))
    out_ref[...] = h + gate * _dot(p_ref[...].astype(BF16), wp_ref[...])


def _ple(h2, p2, ln_pe, w_gate, w_proj, *, tile):
    T, D = h2.shape
    row = lambda i: (i, 0)
    const = lambda i: (0, 0)
    return pl.pallas_call(
        _ple_kernel,
        grid=(T // tile,),
        in_specs=[
            pl.BlockSpec((tile, D), row),
            pl.BlockSpec((tile, p2.shape[1]), row),
            pl.BlockSpec((1, D), const),
            pl.BlockSpec(w_gate.shape, const),
            pl.BlockSpec(w_proj.shape, const),
        ],
        out_specs=pl.BlockSpec((tile, D), row),
        out_shape=jax.ShapeDtypeStruct((T, D), F32),
        compiler_params=pltpu.CompilerParams(
            dimension_semantics=("arbitrary",), vmem_limit_bytes=VMEM_LIMIT),
        name="ple",
    )(h2, p2, ln_pe, w_gate, w_proj)


PEER_EXPERT_BLOCK = 2048


def _pick_tile(n, pref):
    t = min(pref, n)
    assert n % t == 0, (n, t)
    return t


def kernel(x, p, positions, ln_mix, w_in, pool_w, pool_scale, q_norm, k_norm,
           lambda_q1, lambda_k1, lambda_q2, lambda_k2, subln, w_o, ln_ffn,
           w_peer_q, peer_subkeys, peer_u, peer_v, ln_pe, w_pe_gate, w_pe_proj):
    B, S, D = x.shape
    T = B * S
    depth = w_in.shape[0]
    d_pool = pool_scale.shape[-1]
    d_attn = (w_in.shape[-1] - d_pool) // 3
    n_groups = d_attn // HEAD_DIM
    n_exp = peer_u.shape[1]

    lane = jnp.arange(LANES)
    invf = (ROPE_THETA ** (-(2.0 * (lane % (HEAD_DIM // 2))).astype(F32) / HEAD_DIM)).reshape(1, LANES)
    gid = jnp.arange(d_attn) // HEAD_DIM
    bd = (gid[:, None] == gid[None, :]).astype(BF16)

    pos2 = positions.reshape(T, 1)
    h = x.reshape(T, D)
    for i in range(depth):
        lam_init = 0.8 - 0.6 * math.exp(-0.3 * i)
        ypool, q, k, v = _mix_in(
            h, pos2, ln_mix[i][None], w_in[i].astype(BF16), pool_w[i].astype(BF16),
            pool_scale[i][None], jnp.tile(q_norm[i], n_groups)[None],
            jnp.tile(k_norm[i], n_groups)[None], invf, bd,
            batch=B, seq=S, tile=_pick_tile(S, 512))
        attn = functools.partial(_diff_attn, batch=B, seq=S, tq=_pick_tile(S, 512),
                                 lam_init=lam_init)
        score_bound = (HEAD_DIM * QK_SCALE * 1.02
                       * jnp.max(jnp.abs(q_norm[i])) * jnp.max(jnp.abs(k_norm[i])))
        o = lax.cond(score_bound <= EXP2_SAFE_RANGE,
                     functools.partial(attn, bounded=True), functools.partial(attn, bounded=False),
                     q, k, v, lambda_q1[i][None], lambda_k1[i][None], lambda_q2[i][None],
                     lambda_k2[i][None], subln[i][None])
        h1, xn, rank2, eb, quota, ea = _peer_front(
            h, ypool, o, w_o[i].astype(BF16), ln_ffn[i][None], w_peer_q[i].astype(BF16),
            peer_subkeys[i].astype(BF16), tile=_pick_tile(T, 256))
        vt_blocks = (peer_v[i].astype(BF16).reshape(n_exp // PEER_EXPERT_BLOCK, PEER_EXPERT_BLOCK, D)
                     .transpose(0, 2, 1))
        h2 = _peer_dense(xn, h1, peer_u[i].astype(BF16), vt_blocks, rank2, eb, quota, ea,
                         tile=_pick_tile(T, 512))
        h = _ple(h2, p[i].reshape(T, -1), ln_pe[i][None], w_pe_gate[i].astype(BF16),
                 w_pe_proj[i].astype(BF16), tile=_pick_tile(T, 512))
    return h.reshape(B, S, D)
```

```python
import functools
import math

import jax
import jax.numpy as jnp
from jax import lax
from jax.experimental import pallas as pl
from jax.experimental.pallas import tpu as pltpu

EPS = 1e-6
ROPE_THETA = 10000.0
POOL_WINDOWS = (2, 4, 8, 16)
POOL_HALO = 16
HEAD_DIM = 64
QK_SCALE = math.log2(math.e) / math.sqrt(HEAD_DIM)
EXP2_SAFE_RANGE = 60.0
N_KEYS = 128
PEER_TOPK = 16
N_PEER_HEADS = 8
LANES = 128
ROWS_F32 = 8
ROWS_BF16 = 16
GATE_SLAB = 256
VMEM_LIMIT = 56 * 1024 * 1024

BF16 = jnp.bfloat16
F32 = jnp.float32


def _rms(x, gain):
    return x * lax.rsqrt(jnp.mean(x * x, axis=-1, keepdims=True) + EPS) * gain


def _gelu_exact(x):
    return 0.5 * x * (1.0 + lax.erf(x * math.sqrt(0.5)))


def _dot(a, b):
    return jnp.dot(a, b, preferred_element_type=F32)


def _dot_nt(a, b):
    return lax.dot_general(a, b, (((1,), (1,)), ((), ())), preferred_element_type=F32)


def _mix_in_kernel(x_ref, pos_ref, ln_ref, w_in_ref, pool_w_ref, pool_scale_ref,
                   qn_ref, kn_ref, invf_ref, bd_ref,
                   ypool_ref, q_ref, k_ref, v_ref, ext_ref, *, tile):
    s_idx = pl.program_id(1)
    d_pool = ypool_ref.shape[-1]
    d_attn = q_ref.shape[-1]
    group = d_pool // len(POOL_WINDOWS)

    xn = _rms(x_ref[...], ln_ref[...]).astype(BF16)
    z = _dot(xn, w_in_ref[...])

    @pl.when(s_idx == 0)
    def _():
        ext_ref[0:POOL_HALO, :] = jnp.zeros((POOL_HALO, d_pool), F32)

    @pl.when(s_idx != 0)
    def _():
        ext_ref[0:POOL_HALO, :] = ext_ref[tile:tile + POOL_HALO, :]

    zp = z[:, :d_pool]
    ext_ref[POOL_HALO:POOL_HALO + tile, :] = zp
    t_in_seq = s_idx * tile + lax.broadcasted_iota(jnp.int32, (tile, 1), 0)
    for g, w in enumerate(POOL_WINDOWS):
        lo, hi = g * group, (g + 1) * group
        acc = zp[:, lo:hi]
        for k in range(1, w):
            acc = acc + ext_ref[POOL_HALO - k:POOL_HALO - k + tile, lo:hi]
        cnt = jnp.minimum(t_in_seq + 1, w).astype(F32)
        m = acc / cnt - zp[:, lo:hi]
        y = _dot(m.astype(BF16), pool_w_ref[g])
        ypool_ref[:, lo:hi] = (y * pool_scale_ref[:, lo:hi]).astype(ypool_ref.dtype)

    ang = pos_ref[...].astype(F32) * invf_ref[...]
    cos1 = jnp.cos(ang)
    sin1 = jnp.sin(ang)
    lane = lax.broadcasted_iota(jnp.int32, (1, LANES), 1)
    first_half = (lane % HEAD_DIM) < (HEAD_DIM // 2)
    sin1 = jnp.where(first_half, -sin1, sin1)
    reps = d_attn // LANES
    cos = jnp.concatenate([cos1] * reps, axis=1)
    sin = jnp.concatenate([sin1] * reps, axis=1)
    first_half = jnp.concatenate([first_half] * reps, axis=1)

    def norm_rope(t, gain):
        sq = t * t
        hi_part = sq.astype(BF16)
        lo_part = (sq - hi_part.astype(F32)).astype(BF16)
        ss = _dot(hi_part, bd_ref[...]) + _dot(lo_part, bd_ref[...])
        tn = t * lax.rsqrt(ss * (1.0 / HEAD_DIM) + EPS) * gain
        half = HEAD_DIM // 2
        swapped = jnp.where(first_half,
                            pltpu.roll(tn, d_attn - half, 1),
                            pltpu.roll(tn, half, 1))
        return tn * cos + swapped * sin

    zq = z[:, d_pool:d_pool + d_attn]
    zk = z[:, d_pool + d_attn:d_pool + 2 * d_attn]
    q_ref[...] = (norm_rope(zq, qn_ref[...]) * QK_SCALE).astype(q_ref.dtype)
    k_ref[...] = norm_rope(zk, kn_ref[...]).astype(k_ref.dtype)
    ones_col = jnp.where(lane == 0, 1.0, 0.0).astype(v_ref.dtype) + jnp.zeros((tile, LANES), v_ref.dtype)
    v0 = d_pool + 2 * d_attn
    for hh in range(d_attn // LANES):
        v_ref[:, 2 * hh * LANES:(2 * hh + 1) * LANES] = (
            z[:, v0 + hh * LANES:v0 + (hh + 1) * LANES].astype(v_ref.dtype))
        v_ref[:, (2 * hh + 1) * LANES:(2 * hh + 2) * LANES] = ones_col


def _mix_in(x2, pos2, ln_mix, w_in, pool_w, pool_scale, qn_t, kn_t, invf, bd, *, batch, seq, tile):
    T, D = x2.shape
    d_in = w_in.shape[1]
    d_pool = pool_scale.shape[1]
    d_attn = (d_in - d_pool) // 3
    ns = seq // tile
    row = lambda b, s: (b * ns + s, 0)
    const = lambda b, s: (0, 0)
    return pl.pallas_call(
        functools.partial(_mix_in_kernel, tile=tile),
        grid=(batch, ns),
        in_specs=[
            pl.BlockSpec((tile, D), row),
            pl.BlockSpec((tile, 1), row),
            pl.BlockSpec((1, D), const),
            pl.BlockSpec((D, d_in), const),
            pl.BlockSpec(pool_w.shape, lambda b, s: (0, 0, 0)),
            pl.BlockSpec((1, d_pool), const),
            pl.BlockSpec((1, d_attn), const),
            pl.BlockSpec((1, d_attn), const),
            pl.BlockSpec((1, LANES), const),
            pl.BlockSpec((d_attn, d_attn), const),
        ],
        out_specs=[pl.BlockSpec((tile, d_pool), row), pl.BlockSpec((tile, d_attn), row),
                   pl.BlockSpec((tile, d_attn), row), pl.BlockSpec((tile, 2 * d_attn), row)],
        out_shape=[jax.ShapeDtypeStruct((T, d_pool), BF16), jax.ShapeDtypeStruct((T, d_attn), BF16),
                   jax.ShapeDtypeStruct((T, d_attn), BF16),
                   jax.ShapeDtypeStruct((T, 2 * d_attn), BF16)],
        scratch_shapes=[pltpu.VMEM((POOL_HALO + tile, d_pool), F32)],
        compiler_params=pltpu.CompilerParams(
            dimension_semantics=("arbitrary", "arbitrary"), vmem_limit_bytes=VMEM_LIMIT),
        name="mix_in",
    )(x2, pos2, ln_mix, w_in, pool_w, pool_scale, qn_t, kn_t, invf, bd)


def _diff_attn_kernel(q_ref, k_ref, v_ref, lq1_ref, lk1_ref, lq2_ref, lk2_ref, subln_ref,
                      o_ref, acc_ref, m_ref, *, tq, lam_init, bounded):
    qi = pl.program_id(2)
    hw = 2 * HEAD_DIM
    q = q_ref[...]
    lane = lax.broadcasted_iota(jnp.int32, (1, hw), 1)
    zero = jnp.zeros_like(q)
    qs = jnp.concatenate([jnp.where(lane < HEAD_DIM, q, zero),
                          jnp.where(lane >= HEAD_DIM, q, zero)], axis=0)
    acc_ref[...] = jnp.zeros_like(acc_ref)
    if not bounded:
        m_ref[...] = jnp.full(m_ref.shape, -jnp.inf, F32)

    def step(j, masked):
        off = pl.multiple_of(j * tq, tq)
        s = _dot_nt(qs, k_ref[pl.ds(off, tq), :])
        if masked:
            r = lax.broadcasted_iota(jnp.int32, (tq, tq), 0)
            c = lax.broadcasted_iota(jnp.int32, (tq, tq), 1)
            keep = jnp.concatenate([c <= r, c <= r], axis=0)
            s = jnp.where(keep, s, -jnp.inf)
        v_t = v_ref[pl.ds(off, tq), :]
        if bounded:
            acc_ref[...] += _dot(jnp.exp2(s).astype(BF16), v_t)
        else:
            m_prev = m_ref[...]
            m_new = jnp.maximum(m_prev, jnp.max(s, axis=1, keepdims=True))
            m_ref[...] = m_new
            acc_ref[...] = (jnp.exp2(m_prev - m_new) * acc_ref[...]
                            + _dot(jnp.exp2(s - m_new).astype(BF16), v_t))

    def body(j, carry):
        step(j, False)
        return carry

    lax.fori_loop(0, qi, body, 0)
    step(qi, True)

    lam = (jnp.exp(jnp.sum(lq1_ref[...] * lk1_ref[...], axis=1, keepdims=True))
           - jnp.exp(jnp.sum(lq2_ref[...] * lk2_ref[...], axis=1, keepdims=True)) + lam_init)
    acc = acc_ref[...]
    num, den = acc[:, :hw], acc[:, hw:hw + 1]
    o = num[:tq] / den[:tq] - lam * (num[tq:] / den[tq:])
    o_ref[...] = (_rms(o, subln_ref[...]) * (1.0 - lam_init)).astype(o_ref.dtype)


def _diff_attn(q, k, v, lq1, lk1, lq2, lk2, subln, *, batch, seq, tq, lam_init, bounded):
    T, d_attn = q.shape
    heads = d_attn // (2 * HEAD_DIM)
    nq = seq // tq
    hw = 2 * HEAD_DIM
    vec = pl.BlockSpec((1, HEAD_DIM), lambda b, h, i: (0, 0))
    return pl.pallas_call(
        functools.partial(_diff_attn_kernel, tq=tq, lam_init=lam_init, bounded=bounded),
        grid=(batch, heads, nq),
        in_specs=[
            pl.BlockSpec((tq, hw), lambda b, h, i: (b * nq + i, h)),
            pl.BlockSpec((seq, hw), lambda b, h, i: (b, h)),
            pl.BlockSpec((seq, 2 * hw), lambda b, h, i: (b, h)),
            vec, vec, vec, vec,
            pl.BlockSpec((1, hw), lambda b, h, i: (0, 0)),
        ],
        out_specs=pl.BlockSpec((tq, hw), lambda b, h, i: (b * nq + i, h)),
        out_shape=jax.ShapeDtypeStruct((T, d_attn), BF16),
        scratch_shapes=[pltpu.VMEM((2 * tq, 2 * hw), F32), pltpu.VMEM((2 * tq, 1), F32)],
        compiler_params=pltpu.CompilerParams(
            dimension_semantics=("arbitrary", "arbitrary", "arbitrary"),
            vmem_limit_bytes=VMEM_LIMIT),
        name="diff_attn_bounded" if bounded else "diff_attn_online",
    )(q, k, v, lq1, lk1, lq2, lk2, subln)


def _peer_front_kernel(x_ref, ypool_ref, o_ref, wo_ref, ln_ref, wq_ref, sk_ref,
                       h1_ref, xn_ref, rank2_ref, eb_ref, quota_ref, ea_ref,
                       qp_ref, a_ref, b_ref, stage_ref):
    d_pool = ypool_ref.shape[-1]
    tile = x_ref.shape[0]
    h1 = (x_ref[...] + _dot(ypool_ref[...], wo_ref[0:d_pool, :])
          + _dot(o_ref[...], wo_ref[d_pool:, :]))
    h1_ref[...] = h1
    xn = _rms(h1, ln_ref[...]).astype(BF16)
    xn_ref[...] = xn
    qp_ref[...] = _dot(xn, wq_ref[...]).astype(BF16)

    neg = jnp.float32(-jnp.inf)

    def extract_sorted(x, dst_ref, want_rank):
        rank = jnp.full(x.shape, float(PEER_TOPK), F32) if want_rank else None
        for r in range(PEER_TOPK):
            m = jnp.max(x, axis=0, keepdims=True)
            dst_ref[r:r + 1, :] = m
            hit = x == m
            if want_rank:
                rank = jnp.where(hit, float(r), rank)
            x = jnp.where(hit, neg, x)
        return rank

    def chunk(c, carry):
        qp_c = qp_ref[pl.ds(pl.multiple_of(c * LANES, LANES), LANES), :]
        for h in range(N_PEER_HEADS):
            c0 = (2 * h) * N_KEYS
            s1 = _dot_nt(sk_ref[h, 0], qp_c[:, c0:c0 + N_KEYS])
            s2 = _dot_nt(sk_ref[h, 1], qp_c[:, c0 + N_KEYS:c0 + 2 * N_KEYS])
            extract_sorted(s1, a_ref, False)
            rank2 = extract_sorted(s2, b_ref, True)
            a = a_ref[...]
            b = b_ref[...]
            cand = jnp.concatenate(
                [a + b[0:1, :]]
                + [a[0:ROWS_F32, :] + b[q:q + 1, :] for q in range(1, ROWS_F32)]
                + [a[0:1, :] + b[ROWS_F32:, :]], axis=0)
            c_left = cand
            tau = None
            for r in range(PEER_TOPK):
                tau = jnp.max(c_left, axis=0, keepdims=True)
                c_left = jnp.where(c_left == tau, neg, c_left)
            top = a[0:1, :] + b[0:1, :]
            zsum = jnp.sum(jnp.where(cand >= tau, jnp.exp(cand - top), 0.0), axis=0, keepdims=True)
            quota_sorted = jnp.zeros(a.shape, F32)
            for q in range(PEER_TOPK):
                quota_sorted = quota_sorted + jnp.where(a + b[q:q + 1, :] >= tau, 1.0, 0.0)
            quota = jnp.zeros(s1.shape, F32)
            for p in range(PEER_TOPK):
                quota = jnp.where(s1 == a[p:p + 1, :], quota_sorted[p:p + 1, :], quota)
            stage_ref[0, c, h] = rank2
            stage_ref[1, c, h] = jnp.exp(s2 - b[0:1, :])
            stage_ref[2, c, h] = quota
            stage_ref[3, c, h] = jnp.exp(s1 - a[0:1, :]) / zsum
        return carry

    lax.fori_loop(0, tile // LANES, chunk, 0)
    per_slab = GATE_SLAB // LANES
    for k, dst in enumerate((rank2_ref, eb_ref, quota_ref, ea_ref)):
        for c in range(tile // LANES):
            lanes = slice((c % per_slab) * LANES, (c % per_slab + 1) * LANES)
            dst[c // per_slab, :, :, lanes] = stage_ref[k, c].astype(dst.dtype)


def _peer_front(x2, ypool, o, w_o, ln_ffn, w_q, subkeys, *, tile):
    T, D = x2.shape
    d_pool = ypool.shape[1]
    nh = subkeys.shape[0]
    row = lambda i: (i, 0)
    const = lambda i: (0, 0)
    tok = pl.BlockSpec((tile // GATE_SLAB, nh, N_KEYS, GATE_SLAB), lambda i: (i, 0, 0, 0))
    gate_shape = (T // GATE_SLAB, nh, N_KEYS, GATE_SLAB)
    return pl.pallas_call(
        _peer_front_kernel,
        grid=(T // tile,),
        in_specs=[
            pl.BlockSpec((tile, D), row),
            pl.BlockSpec((tile, d_pool), row),
            pl.BlockSpec((tile, o.shape[1]), row),
            pl.BlockSpec(w_o.shape, const),
            pl.BlockSpec((1, D), const),
            pl.BlockSpec(w_q.shape, const),
            pl.BlockSpec(subkeys.shape, lambda i: (0, 0, 0, 0)),
        ],
        out_specs=[pl.BlockSpec((tile, D), row), pl.BlockSpec((tile, D), row), tok, tok, tok, tok],
        out_shape=[
            jax.ShapeDtypeStruct((T, D), F32),
            jax.ShapeDtypeStruct((T, D), BF16),
            jax.ShapeDtypeStruct(gate_shape, BF16),
            jax.ShapeDtypeStruct(gate_shape, BF16),
            jax.ShapeDtypeStruct(gate_shape, F32),
            jax.ShapeDtypeStruct(gate_shape, F32),
        ],
        scratch_shapes=[
            pltpu.VMEM((tile, w_q.shape[1]), BF16),
            pltpu.VMEM((PEER_TOPK, LANES), F32),
            pltpu.VMEM((PEER_TOPK, LANES), F32),
            pltpu.VMEM((4, tile // LANES, nh, N_KEYS, LANES), F32),
        ],
        compiler_params=pltpu.CompilerParams(
            dimension_semantics=("arbitrary",), vmem_limit_bytes=VMEM_LIMIT),
        name="peer_front",
    )(x2, ypool, o, w_o, ln_ffn, w_q, subkeys)


def _peer_dense_kernel(xn_ref, h1_ref, u_ref, vt_ref, rank2_ref, eb_ref, quota_ref, ea_ref,
                       out_ref, acc_ref, hid_ref, gate_ref, xnt_ref, *, keys_per_block):
    e_idx = pl.program_id(1)
    n_blocks = pl.num_programs(1) - 1
    tile = xn_ref.shape[0]
    n_row_chunks = N_KEYS // ROWS_BF16
    slot = e_idx % 2

    @pl.when(e_idx == 0)
    def _():
        acc_ref[...] = jnp.zeros_like(acc_ref)
        hid_ref[1] = jnp.zeros(hid_ref.shape[1:], hid_ref.dtype)
        xnt_ref[...] = xn_ref[...].T

    @pl.when(e_idx < n_blocks)
    def _():
        for ii in range(keys_per_block):
            i = e_idx * keys_per_block + ii
            for tc in range(tile // GATE_SLAB):
                lanes = slice(tc * GATE_SLAB, (tc + 1) * GATE_SLAB)
                gates = [jnp.zeros((ROWS_BF16, GATE_SLAB), BF16)] * n_row_chunks
                for h in range(N_PEER_HEADS):
                    quota = jnp.broadcast_to(quota_ref[tc, h, pl.ds(i, 1), :],
                                             (ROWS_BF16, GATE_SLAB)).astype(BF16)
                    ea = jnp.broadcast_to(ea_ref[tc, h, pl.ds(i, 1), :],
                                          (ROWS_BF16, GATE_SLAB)).astype(BF16)
                    for c in range(n_row_chunks):
                        rows = slice(c * ROWS_BF16, (c + 1) * ROWS_BF16)
                        w = eb_ref[tc, h, rows, :] * ea
                        gates[c] = gates[c] + jnp.where(rank2_ref[tc, h, rows, :] < quota, w,
                                                        jnp.zeros_like(w))
                for c in range(n_row_chunks):
                    r0 = ii * N_KEYS + c * ROWS_BF16
                    gate_ref[r0:r0 + ROWS_BF16, lanes] = gates[c]
        a_t = _dot(u_ref[...], xnt_ref[...])
        acc_ref[...] += _dot(vt_ref[...], hid_ref[1 - slot])
        hid_ref[slot] = _gelu_exact(a_t).astype(BF16) * gate_ref[...]

    @pl.when(e_idx == n_blocks)
    def _():
        acc = acc_ref[...] + _dot(vt_ref[...], hid_ref[1 - slot])
        out_ref[...] = h1_ref[...] + acc.T


def _peer_dense(xn, h1, u, vt_blocks, rank2, eb, quota, ea, *, tile):
    T, D = xn.shape
    n_blocks, _, eblk = vt_blocks.shape
    nh = rank2.shape[1]
    row = lambda t, e: (t, 0)
    tok = pl.BlockSpec((tile // GATE_SLAB, nh, N_KEYS, GATE_SLAB), lambda t, e: (t, 0, 0, 0))
    return pl.pallas_call(
        functools.partial(_peer_dense_kernel, keys_per_block=eblk // N_KEYS),
        grid=(T // tile, n_blocks + 1),
        in_specs=[
            pl.BlockSpec((tile, D), row),
            pl.BlockSpec((tile, D), row),
            pl.BlockSpec((eblk, D), lambda t, e: (jnp.minimum(e, n_blocks - 1), 0)),
            pl.BlockSpec((None, D, eblk), lambda t, e: (jnp.maximum(e - 1, 0), 0, 0)),
            tok, tok, tok, tok,
        ],
        out_specs=pl.BlockSpec((tile, D), row),
        out_shape=jax.ShapeDtypeStruct((T, D), F32),
        scratch_shapes=[pltpu.VMEM((D, tile), F32), pltpu.VMEM((2, eblk, tile), BF16),
                        pltpu.VMEM((eblk, tile), BF16), pltpu.VMEM((D, tile), BF16)],
        compiler_params=pltpu.CompilerParams(
            dimension_semantics=("arbitrary", "arbitrary"), vmem_limit_bytes=VMEM_LIMIT),
        name="peer_dense",
    )(xn, h1, u, vt_blocks, rank2, eb, quota, ea)


def _ple_kernel(h_ref, p_ref, ln_ref, wg_ref, wp_ref, out_ref):
    h = h_ref[...]
    gate = jax.nn.sigmoid(_dot(_rms(h, ln_ref[...]).astype(BF16), wg_ref[...]))
    out_ref[...] = h + gate * _dot(p_ref[...].astype(BF16), wp_ref[...])


def _ple(h2, p2, ln_pe, w_gate, w_proj, *, tile):
    T, D = h2.shape
    row = lambda i: (i, 0)
    const = lambda i: (0, 0)
    return pl.pallas_call(
        _ple_kernel,
        grid=(T // tile,),
        in_specs=[
            pl.BlockSpec((tile, D), row),
            pl.BlockSpec((tile, p2.shape[1]), row),
            pl.BlockSpec((1, D), const),
            pl.BlockSpec(w_gate.shape, const),
            pl.BlockSpec(w_proj.shape, const),
        ],
        out_specs=pl.BlockSpec((tile, D), row),
        out_shape=jax.ShapeDtypeStruct((T, D), F32),
        compiler_params=pltpu.CompilerParams(
            dimension_semantics=("arbitrary",), vmem_limit_bytes=VMEM_LIMIT),
        name="ple",
    )(h2, p2, ln_pe, w_gate, w_proj)


PEER_EXPERT_BLOCK = 2048


def _pick_tile(n, pref):
    t = min(pref, n)
    assert n % t == 0, (n, t)
    return t


def kernel(x, p, positions, ln_mix, w_in, pool_w, pool_scale, q_norm, k_norm,
           lambda_q1, lambda_k1, lambda_q2, lambda_k2, subln, w_o, ln_ffn,
           w_peer_q, peer_subkeys, peer_u, peer_v, ln_pe, w_pe_gate, w_pe_proj):
    B, S, D = x.shape
    T = B * S
    depth = w_in.shape[0]
    d_pool = pool_scale.shape[-1]
    d_attn = (w_in.shape[-1] - d_pool) // 3
    n_groups = d_attn // HEAD_DIM
    n_exp = peer_u.shape[1]

    lane = jnp.arange(LANES)
    invf = (ROPE_THETA ** (-(2.0 * (lane % (HEAD_DIM // 2))).astype(F32) / HEAD_DIM)).reshape(1, LANES)
    gid = jnp.arange(d_attn) // HEAD_DIM
    bd = (gid[:, None] == gid[None, :]).astype(BF16)

    pos2 = positions.reshape(T, 1)
    h = x.reshape(T, D)
    for i in range(depth):
        lam_init = 0.8 - 0.6 * math.exp(-0.3 * i)
        ypool, q, k, v = _mix_in(
            h, pos2, ln_mix[i][None], w_in[i].astype(BF16), pool_w[i].astype(BF16),
            pool_scale[i][None], jnp.tile(q_norm[i], n_groups)[None],
            jnp.tile(k_norm[i], n_groups)[None], invf, bd,
            batch=B, seq=S, tile=_pick_tile(S, 512))
        attn = functools.partial(_diff_attn, batch=B, seq=S, tq=_pick_tile(S, 512),
                                 lam_init=lam_init)
        score_bound = (HEAD_DIM * QK_SCALE * 1.02
                       * jnp.max(jnp.abs(q_norm[i])) * jnp.max(jnp.abs(k_norm[i])))
        o = lax.cond(score_bound <= EXP2_SAFE_RANGE,
                     functools.partial(attn, bounded=True), functools.partial(attn, bounded=False),
                     q, k, v, lambda_q1[i][None], lambda_k1[i][None], lambda_q2[i][None],
                     lambda_k2[i][None], subln[i][None])
        h1, xn, rank2, eb, quota, ea = _peer_front(
            h, ypool, o, w_o[i].astype(BF16), ln_ffn[i][None], w_peer_q[i].astype(BF16),
            peer_subkeys[i].astype(BF16), tile=_pick_tile(T, 256))
        vt_blocks = (peer_v[i].astype(BF16).reshape(n_exp // PEER_EXPERT_BLOCK, PEER_EXPERT_BLOCK, D)
                     .transpose(0, 2, 1))
        h2 = _peer_dense(xn, h1, peer_u[i].astype(BF16), vt_blocks, rank2, eb, quota, ea,
                         tile=_pick_tile(T, 512))
        h = _ple(h2, p[i].reshape(T, -1), ln_pe[i][None], w_pe_gate[i].astype(BF16),
                 w_pe_proj[i].astype(BF16), tile=_pick_tile(T, 512))
    return h.reshape(B, S, D)
```

```python
import functools
import math

import jax
import jax.numpy as jnp
from jax import lax
from jax.experimental import pallas as pl
from jax.experimental.pallas import tpu as pltpu

EPS = 1e-6
ROPE_THETA = 10000.0
POOL_WINDOWS = (2, 4, 8, 16)
POOL_HALO = 16
HEAD_DIM = 64
QK_SCALE = math.log2(math.e) / math.sqrt(HEAD_DIM)
EXP2_SAFE_RANGE = 60.0
N_KEYS = 128
PEER_TOPK = 16
N_PEER_HEADS = 8
LANES = 128
ROWS_F32 = 8
ROWS_BF16 = 16
GATE_SLAB = 256
GATE_KEYS = 2
VMEM_LIMIT = 56 * 1024 * 1024

BF16 = jnp.bfloat16
F32 = jnp.float32


def _rms(x, gain):
    return x * lax.rsqrt(jnp.mean(x * x, axis=-1, keepdims=True) + EPS) * gain


def _gelu_exact(x):
    return 0.5 * x * (1.0 + lax.erf(x * math.sqrt(0.5)))


def _dot(a, b):
    return jnp.dot(a, b, preferred_element_type=F32)


def _dot_nt(a, b):
    return lax.dot_general(a, b, (((1,), (1,)), ((), ())), preferred_element_type=F32)


def _mix_in_kernel(x_ref, pos_ref, ln_ref, w_in_ref, pool_w_ref, pool_scale_ref,
                   qn_ref, kn_ref, invf_ref, bd_ref,
                   ypool_ref, q_ref, k_ref, v_ref, ext_ref, *, tile):
    s_idx = pl.program_id(1)
    d_pool = ypool_ref.shape[-1]
    d_attn = q_ref.shape[-1]
    group = d_pool // len(POOL_WINDOWS)

    xn = _rms(x_ref[...], ln_ref[...]).astype(BF16)
    z = _dot(xn, w_in_ref[...])

    @pl.when(s_idx == 0)
    def _():
        ext_ref[0:POOL_HALO, :] = jnp.zeros((POOL_HALO, d_pool), F32)

    @pl.when(s_idx != 0)
    def _():
        ext_ref[0:POOL_HALO, :] = ext_ref[tile:tile + POOL_HALO, :]

    zp = z[:, :d_pool]
    ext_ref[POOL_HALO:POOL_HALO + tile, :] = zp
    t_in_seq = s_idx * tile + lax.broadcasted_iota(jnp.int32, (tile, 1), 0)
    for g, w in enumerate(POOL_WINDOWS):
        lo, hi = g * group, (g + 1) * group
        acc = zp[:, lo:hi]
        for k in range(1, w):
            acc = acc + ext_ref[POOL_HALO - k:POOL_HALO - k + tile, lo:hi]
        cnt = jnp.minimum(t_in_seq + 1, w).astype(F32)
        m = acc / cnt - zp[:, lo:hi]
        y = _dot(m.astype(BF16), pool_w_ref[g])
        ypool_ref[:, lo:hi] = (y * pool_scale_ref[:, lo:hi]).astype(ypool_ref.dtype)

    ang = pos_ref[...].astype(F32) * invf_ref[...]
    cos1 = jnp.cos(ang)
    sin1 = jnp.sin(ang)
    lane = lax.broadcasted_iota(jnp.int32, (1, LANES), 1)
    first_half = (lane % HEAD_DIM) < (HEAD_DIM // 2)
    sin1 = jnp.where(first_half, -sin1, sin1)
    reps = d_attn // LANES
    cos = jnp.concatenate([cos1] * reps, axis=1)
    sin = jnp.concatenate([sin1] * reps, axis=1)
    first_half = jnp.concatenate([first_half] * reps, axis=1)

    def norm_rope(t, gain):
        sq = t * t
        hi_part = sq.astype(BF16)
        lo_part = (sq - hi_part.astype(F32)).astype(BF16)
        ss = _dot(hi_part, bd_ref[...]) + _dot(lo_part, bd_ref[...])
        tn = t * lax.rsqrt(ss * (1.0 / HEAD_DIM) + EPS) * gain
        half = HEAD_DIM // 2
        swapped = jnp.where(first_half,
                            pltpu.roll(tn, d_attn - half, 1),
                            pltpu.roll(tn, half, 1))
        return tn * cos + swapped * sin

    zq = z[:, d_pool:d_pool + d_attn]
    zk = z[:, d_pool + d_attn:d_pool + 2 * d_attn]
    q_ref[...] = (norm_rope(zq, qn_ref[...]) * QK_SCALE).astype(q_ref.dtype)
    k_ref[...] = norm_rope(zk, kn_ref[...]).astype(k_ref.dtype)
    ones_col = jnp.where(lane == 0, 1.0, 0.0).astype(v_ref.dtype) + jnp.zeros((tile, LANES), v_ref.dtype)
    v0 = d_pool + 2 * d_attn
    for hh in range(d_attn // LANES):
        v_ref[:, 2 * hh * LANES:(2 * hh + 1) * LANES] = (
            z[:, v0 + hh * LANES:v0 + (hh + 1) * LANES].astype(v_ref.dtype))
        v_ref[:, (2 * hh + 1) * LANES:(2 * hh + 2) * LANES] = ones_col


def _mix_in(x2, pos2, ln_mix, w_in, pool_w, pool_scale, qn_t, kn_t, invf, bd, *, batch, seq, tile):
    T, D = x2.shape
    d_in = w_in.shape[1]
    d_pool = pool_scale.shape[1]
    d_attn = (d_in - d_pool) // 3
    ns = seq // tile
    row = lambda b, s: (b * ns + s, 0)
    const = lambda b, s: (0, 0)
    return pl.pallas_call(
        functools.partial(_mix_in_kernel, tile=tile),
        grid=(batch, ns),
        in_specs=[
            pl.BlockSpec((tile, D), row),
            pl.BlockSpec((tile, 1), row),
            pl.BlockSpec((1, D), const),
            pl.BlockSpec((D, d_in), const),
            pl.BlockSpec(pool_w.shape, lambda b, s: (0, 0, 0)),
            pl.BlockSpec((1, d_pool), const),
            pl.BlockSpec((1, d_attn), const),
            pl.BlockSpec((1, d_attn), const),
            pl.BlockSpec((1, LANES), const),
            pl.BlockSpec((d_attn, d_attn), const),
        ],
        out_specs=[pl.BlockSpec((tile, d_pool), row), pl.BlockSpec((tile, d_attn), row),
                   pl.BlockSpec((tile, d_attn), row), pl.BlockSpec((tile, 2 * d_attn), row)],
        out_shape=[jax.ShapeDtypeStruct((T, d_pool), BF16), jax.ShapeDtypeStruct((T, d_attn), BF16),
                   jax.ShapeDtypeStruct((T, d_attn), BF16),
                   jax.ShapeDtypeStruct((T, 2 * d_attn), BF16)],
        scratch_shapes=[pltpu.VMEM((POOL_HALO + tile, d_pool), F32)],
        compiler_params=pltpu.CompilerParams(
            dimension_semantics=("arbitrary", "arbitrary"), vmem_limit_bytes=VMEM_LIMIT),
        name="mix_in",
    )(x2, pos2, ln_mix, w_in, pool_w, pool_scale, qn_t, kn_t, invf, bd)


def _diff_attn_kernel(q_ref, k_ref, v_ref, lq1_ref, lk1_ref, lq2_ref, lk2_ref, subln_ref,
                      o_ref, acc_ref, m_ref, *, tq, lam_init, bounded):
    qi = pl.program_id(2)
    hw = 2 * HEAD_DIM
    q = q_ref[...]
    lane = lax.broadcasted_iota(jnp.int32, (1, hw), 1)
    zero = jnp.zeros_like(q)
    qs = jnp.concatenate([jnp.where(lane < HEAD_DIM, q, zero),
                          jnp.where(lane >= HEAD_DIM, q, zero)], axis=0)
    acc_ref[...] = jnp.zeros_like(acc_ref)
    if not bounded:
        m_ref[...] = jnp.full(m_ref.shape, -jnp.inf, F32)

    def step(j, width, masked):
        off = pl.multiple_of(j * tq, tq)
        s = _dot_nt(qs, k_ref[pl.ds(off, width), :])
        if masked:
            r = lax.broadcasted_iota(jnp.int32, (tq, tq), 0)
            c = lax.broadcasted_iota(jnp.int32, (tq, tq), 1)
            keep = jnp.concatenate([c <= r, c <= r], axis=0)
            s = jnp.where(keep, s, -jnp.inf)
        v_t = v_ref[pl.ds(off, width), :]
        if bounded:
            acc_ref[...] += _dot(jnp.exp2(s).astype(BF16), v_t)
        else:
            m_prev = m_ref[...]
            m_new = jnp.maximum(m_prev, jnp.max(s, axis=1, keepdims=True))
            m_ref[...] = m_new
            acc_ref[...] = (jnp.exp2(m_prev - m_new) * acc_ref[...]
                            + _dot(jnp.exp2(s - m_new).astype(BF16), v_t))

    def body(jj, carry):
        step(2 * jj, 2 * tq, False)
        return carry

    lax.fori_loop(0, qi // 2, body, 0)

    @pl.when(qi % 2 == 1)
    def _():
        step(qi - 1, tq, False)

    step(qi, tq, True)

    lam = (jnp.exp(jnp.sum(lq1_ref[...] * lk1_ref[...], axis=1, keepdims=True))
           - jnp.exp(jnp.sum(lq2_ref[...] * lk2_ref[...], axis=1, keepdims=True)) + lam_init)
    acc = acc_ref[...]
    num, den = acc[:, :hw], acc[:, hw:hw + 1]
    o = num[:tq] / den[:tq] - lam * (num[tq:] / den[tq:])
    o_ref[...] = (_rms(o, subln_ref[...]) * (1.0 - lam_init)).astype(o_ref.dtype)


def _diff_attn(q, k, v, lq1, lk1, lq2, lk2, subln, *, batch, seq, tq, lam_init, bounded):
    T, d_attn = q.shape
    heads = d_attn // (2 * HEAD_DIM)
    nq = seq // tq
    hw = 2 * HEAD_DIM
    vec = pl.BlockSpec((1, HEAD_DIM), lambda b, h, i: (0, 0))
    return pl.pallas_call(
        functools.partial(_diff_attn_kernel, tq=tq, lam_init=lam_init, bounded=bounded),
        grid=(batch, heads, nq),
        in_specs=[
            pl.BlockSpec((tq, hw), lambda b, h, i: (b * nq + i, h)),
            pl.BlockSpec((seq, hw), lambda b, h, i: (b, h)),
            pl.BlockSpec((seq, 2 * hw), lambda b, h, i: (b, h)),
            vec, vec, vec, vec,
            pl.BlockSpec((1, hw), lambda b, h, i: (0, 0)),
        ],
        out_specs=pl.BlockSpec((tq, hw), lambda b, h, i: (b * nq + i, h)),
        out_shape=jax.ShapeDtypeStruct((T, d_attn), BF16),
        scratch_shapes=[pltpu.VMEM((2 * tq, 2 * hw), F32), pltpu.VMEM((2 * tq, 1), F32)],
        compiler_params=pltpu.CompilerParams(
            dimension_semantics=("arbitrary", "arbitrary", "arbitrary"),
            vmem_limit_bytes=VMEM_LIMIT),
        name="diff_attn_bounded" if bounded else "diff_attn_online",
    )(q, k, v, lq1, lk1, lq2, lk2, subln)


def _oddeven_merge(lo, hi, r):
    step = r * 2
    if step < hi - lo:
        yield from _oddeven_merge(lo, hi, step)
        yield from _oddeven_merge(lo + r, hi, step)
        for i in range(lo + r, hi - r, step):
            yield (i, i + r)
    else:
        yield (lo, lo + r)


def _oddeven_merge_sort(lo, hi):
    if hi - lo >= 1:
        mid = lo + (hi - lo) // 2
        yield from _oddeven_merge_sort(lo, mid)
        yield from _oddeven_merge_sort(mid + 1, hi)
        yield from _oddeven_merge(lo, hi, 1)


_SORT16_NETWORK = tuple(_oddeven_merge_sort(0, 15))


def _peer_front_kernel(x_ref, ypool_ref, o_ref, wo_ref, ln_ref, wq_ref, sk_ref,
                       h1_ref, xn_ref, er_ref, quota_ref, ea_ref,
                       qp_ref, a_ref, b_ref, cand_ref, stage_ref):
    d_pool = ypool_ref.shape[-1]
    tile = x_ref.shape[0]
    h1 = (x_ref[...] + _dot(ypool_ref[...], wo_ref[0:d_pool, :])
          + _dot(o_ref[...], wo_ref[d_pool:, :]))
    h1_ref[...] = h1
    xn = _rms(h1, ln_ref[...]).astype(BF16)
    xn_ref[...] = xn
    qp_ref[...] = _dot(xn, wq_ref[...]).astype(BF16)

    neg = jnp.float32(-jnp.inf)

    def top_sorted(x, dst_ref):
        n = x.shape[0] // ROWS_F32
        v = [x[k * ROWS_F32:(k + 1) * ROWS_F32, :] for k in range(n)]
        for i, j in _SORT16_NETWORK:
            if j < n:
                v[i], v[j] = jnp.maximum(v[i], v[j]), jnp.minimum(v[i], v[j])
        m = None
        for r in range(PEER_TOPK):
            m = jnp.max(v[0], axis=0, keepdims=True)
            dst_ref[r:r + 1, :] = m
            depth = min(n, PEER_TOPK - r)
            hit = v[0] == m
            for d in range(depth - 1):
                v[d] = jnp.where(hit, v[d + 1], v[d])
            v[depth - 1] = jnp.where(hit, neg, v[depth - 1])
        return m

    def chunk(c, carry):
        qp_c = qp_ref[pl.ds(pl.multiple_of(c * LANES, LANES), LANES), :]
        for h in range(N_PEER_HEADS):
            c0 = (2 * h) * N_KEYS
            s1 = _dot_nt(sk_ref[h, 0], qp_c[:, c0:c0 + N_KEYS])
            s2 = _dot_nt(sk_ref[h, 1], qp_c[:, c0 + N_KEYS:c0 + 2 * N_KEYS])
            top_sorted(s1, a_ref)
            top_sorted(s2, b_ref)
            a = a_ref[...]
            b = b_ref[...]
            rank2 = jnp.full(s2.shape, float(PEER_TOPK), F32)
            for q in range(PEER_TOPK):
                rank2 = jnp.where(s2 == b[q:q + 1, :], float(q), rank2)
            cand = jnp.concatenate(
                [a + b[0:1, :]]
                + [a[0:ROWS_F32, :] + b[q:q + 1, :] for q in range(1, ROWS_F32)]
                + [a[0:1, :] + b[ROWS_F32:, :]], axis=0)
            tau = top_sorted(cand, cand_ref)
            top = a[0:1, :] + b[0:1, :]
            zsum = jnp.sum(jnp.where(cand >= tau, jnp.exp(cand - top), 0.0), axis=0, keepdims=True)
            quota_sorted = jnp.zeros(a.shape, F32)
            for q in range(PEER_TOPK):
                quota_sorted = quota_sorted + jnp.where(a + b[q:q + 1, :] >= tau, 1.0, 0.0)
            quota = jnp.zeros(s1.shape, F32)
            for p in range(PEER_TOPK):
                quota = jnp.where(s1 == a[p:p + 1, :], quota_sorted[p:p + 1, :], quota)
            stage_ref[0, c, h] = rank2
            stage_ref[1, c, h] = jnp.exp(s2 - b[0:1, :])
            stage_ref[2, c, h] = quota
            stage_ref[3, c, h] = jnp.exp(s1 - a[0:1, :]) / zsum
        return carry

    lax.fori_loop(0, tile // LANES, chunk, 0)
    per_slab = GATE_SLAB // LANES
    n_row_chunks = N_KEYS // ROWS_BF16
    for c in range(tile // LANES):
        lanes = slice((c % per_slab) * LANES, (c % per_slab + 1) * LANES)
        quota_ref[c // per_slab, :, :, lanes] = stage_ref[2, c]
        ea_ref[c // per_slab, :, :, lanes] = stage_ref[3, c]
        for k in range(2):
            er_ref[c // per_slab, :, :, k, :, lanes] = stage_ref[k, c].reshape(
                N_PEER_HEADS, n_row_chunks, ROWS_BF16, LANES).astype(er_ref.dtype)


def _peer_front(x2, ypool, o, w_o, ln_ffn, w_q, subkeys, *, tile):
    T, D = x2.shape
    d_pool = ypool.shape[1]
    nh = subkeys.shape[0]
    row = lambda i: (i, 0)
    const = lambda i: (0, 0)
    tok = pl.BlockSpec((tile // GATE_SLAB, nh, N_KEYS, GATE_SLAB), lambda i: (i, 0, 0, 0))
    gate_shape = (T // GATE_SLAB, nh, N_KEYS, GATE_SLAB)
    n_row_chunks = N_KEYS // ROWS_BF16
    er_tail = (nh, n_row_chunks, 2, ROWS_BF16, GATE_SLAB)
    er_spec = pl.BlockSpec((tile // GATE_SLAB,) + er_tail, lambda i: (i, 0, 0, 0, 0, 0))
    return pl.pallas_call(
        _peer_front_kernel,
        grid=(T // tile,),
        in_specs=[
            pl.BlockSpec((tile, D), row),
            pl.BlockSpec((tile, d_pool), row),
            pl.BlockSpec((tile, o.shape[1]), row),
            pl.BlockSpec(w_o.shape, const),
            pl.BlockSpec((1, D), const),
            pl.BlockSpec(w_q.shape, const),
            pl.BlockSpec(subkeys.shape, lambda i: (0, 0, 0, 0)),
        ],
        out_specs=[pl.BlockSpec((tile, D), row), pl.BlockSpec((tile, D), row), er_spec, tok, tok],
        out_shape=[
            jax.ShapeDtypeStruct((T, D), F32),
            jax.ShapeDtypeStruct((T, D), BF16),
            jax.ShapeDtypeStruct((T // GATE_SLAB,) + er_tail, BF16),
            jax.ShapeDtypeStruct(gate_shape, F32),
            jax.ShapeDtypeStruct(gate_shape, F32),
        ],
        scratch_shapes=[
            pltpu.VMEM((tile, w_q.shape[1]), BF16),
            pltpu.VMEM((PEER_TOPK, LANES), F32),
            pltpu.VMEM((PEER_TOPK, LANES), F32),
            pltpu.VMEM((PEER_TOPK, LANES), F32),
            pltpu.VMEM((4, tile // LANES, nh, N_KEYS, LANES), F32),
        ],
        compiler_params=pltpu.CompilerParams(
            dimension_semantics=("arbitrary",), vmem_limit_bytes=VMEM_LIMIT),
        name="peer_front",
    )(x2, ypool, o, w_o, ln_ffn, w_q, subkeys)


def _peer_dense_kernel(xn_ref, h1_ref, u_ref, vt_ref, er_ref, quota_ref, ea_ref,
                       out_ref, acc_ref, hid_ref, gate_ref, *, keys_per_block):
    e_idx = pl.program_id(1)
    n_blocks = pl.num_programs(1) - 1
    tile = xn_ref.shape[0]
    n_row_chunks = N_KEYS // ROWS_BF16
    slot = e_idx % 2

    @pl.when(e_idx == 0)
    def _():
        acc_ref[...] = jnp.zeros_like(acc_ref)
        hid_ref[1] = jnp.zeros(hid_ref.shape[1:], hid_ref.dtype)

    @pl.when(e_idx < n_blocks)
    def _():
        for ii0 in range(0, keys_per_block, GATE_KEYS):
            for tc in range(tile // GATE_SLAB):
                lanes = slice(tc * GATE_SLAB, (tc + 1) * GATE_SLAB)
                gates = [[jnp.zeros((ROWS_BF16, GATE_SLAB), BF16)] * n_row_chunks
                         for _ in range(GATE_KEYS)]
                for h in range(N_PEER_HEADS):
                    quota, ea = [], []
                    for k in range(GATE_KEYS):
                        i = e_idx * keys_per_block + ii0 + k
                        quota.append(jnp.broadcast_to(quota_ref[tc, h, pl.ds(i, 1), :],
                                                      (ROWS_BF16, GATE_SLAB)).astype(BF16))
                        ea.append(jnp.broadcast_to(ea_ref[tc, h, pl.ds(i, 1), :],
                                                   (ROWS_BF16, GATE_SLAB)).astype(BF16))
                    for c in range(n_row_chunks):
                        rank = er_ref[tc, h, c, 0]
                        weight = er_ref[tc, h, c, 1]
                        for k in range(GATE_KEYS):
                            w = weight * ea[k]
                            gates[k][c] = gates[k][c] + jnp.where(rank < quota[k], w,
                                                                  jnp.zeros_like(w))
                for k in range(GATE_KEYS):
                    for c in range(n_row_chunks):
                        r0 = (ii0 + k) * N_KEYS + c * ROWS_BF16
                        gate_ref[r0:r0 + ROWS_BF16, lanes] = gates[k][c]
        a_t = _dot_nt(u_ref[...], xn_ref[...])
        acc_ref[...] += _dot(vt_ref[...], hid_ref[1 - slot])
        hid_ref[slot] = _gelu_exact(a_t.astype(BF16)) * gate_ref[...]

    @pl.when(e_idx == n_blocks)
    def _():
        acc = acc_ref[...] + _dot(vt_ref[...], hid_ref[1 - slot])
        out_ref[...] = h1_ref[...] + acc.T


def _peer_dense(xn, h1, u, vt_blocks, er, quota, ea, *, tile):
    T, D = xn.shape
    n_blocks, _, eblk = vt_blocks.shape
    nh = quota.shape[1]
    row = lambda t, e: (t, 0)
    tok = pl.BlockSpec((tile // GATE_SLAB, nh, N_KEYS, GATE_SLAB), lambda t, e: (t, 0, 0, 0))
    er_spec = pl.BlockSpec((tile // GATE_SLAB,) + er.shape[1:], lambda t, e: (t, 0, 0, 0, 0, 0))
    return pl.pallas_call(
        functools.partial(_peer_dense_kernel, keys_per_block=eblk // N_KEYS),
        grid=(T // tile, n_blocks + 1),
        in_specs=[
            pl.BlockSpec((tile, D), row),
            pl.BlockSpec((tile, D), row),
            pl.BlockSpec((eblk, D), lambda t, e: (jnp.minimum(e, n_blocks - 1), 0)),
            pl.BlockSpec((None, D, eblk), lambda t, e: (jnp.maximum(e - 1, 0), 0, 0)),
            er_spec, tok, tok,
        ],
        out_specs=pl.BlockSpec((tile, D), row),
        out_shape=jax.ShapeDtypeStruct((T, D), F32),
        scratch_shapes=[pltpu.VMEM((D, tile), F32), pltpu.VMEM((2, eblk, tile), BF16),
                        pltpu.VMEM((eblk, tile), BF16)],
        compiler_params=pltpu.CompilerParams(
            dimension_semantics=("arbitrary", "arbitrary"), vmem_limit_bytes=VMEM_LIMIT),
        name="peer_dense",
    )(xn, h1, u, vt_blocks, er, quota, ea)


def _ple_kernel(h_ref, p_ref, ln_ref, wg_ref, wp_ref, out_ref):
    h = h_ref[...]
    gate = jax.nn.sigmoid(_dot(_rms(h, ln_ref[...]).astype(BF16), wg_ref[...]))
    out_ref[...] = h + gate * _dot(p_ref[...].astype(BF16), wp_ref[...])


def _ple(h2, p2, ln_pe, w_gate, w_proj, *, tile):
    T, D = h2.shape
    row = lambda i: (i, 0)
    const = lambda i: (0, 0)
    return pl.pallas_call(
        _ple_kernel,
        grid=(T // tile,),
        in_specs=[
            pl.BlockSpec((tile, D), row),
            pl.BlockSpec((tile, p2.shape[1]), row),
            pl.BlockSpec((1, D), const),
            pl.BlockSpec(w_gate.shape, const),
            pl.BlockSpec(w_proj.shape, const),
        ],
        out_specs=pl.BlockSpec((tile, D), row),
        out_shape=jax.ShapeDtypeStruct((T, D), F32),
        compiler_params=pltpu.CompilerParams(
            dimension_semantics=("arbitrary",), vmem_limit_bytes=VMEM_LIMIT),
        name="ple",
    )(h2, p2, ln_pe, w_gate, w_proj)


PEER_EXPERT_BLOCK = 2048


def _pick_tile(n, pref):
    t = min(pref, n)
    assert n % t == 0, (n, t)
    return t


def kernel(x, p, positions, ln_mix, w_in, pool_w, pool_scale, q_norm, k_norm,
           lambda_q1, lambda_k1, lambda_q2, lambda_k2, subln, w_o, ln_ffn,
           w_peer_q, peer_subkeys, peer_u, peer_v, ln_pe, w_pe_gate, w_pe_proj):
    B, S, D = x.shape
    T = B * S
    depth = w_in.shape[0]
    d_pool = pool_scale.shape[-1]
    d_attn = (w_in.shape[-1] - d_pool) // 3
    n_groups = d_attn // HEAD_DIM
    n_exp = peer_u.shape[1]

    lane = jnp.arange(LANES)
    invf = (ROPE_THETA ** (-(2.0 * (lane % (HEAD_DIM // 2))).astype(F32) / HEAD_DIM)).reshape(1, LANES)
    gid = jnp.arange(d_attn) // HEAD_DIM
    bd = (gid[:, None] == gid[None, :]).astype(BF16)

    pos2 = positions.reshape(T, 1)
    h = x.reshape(T, D)
    for i in range(depth):
        lam_init = 0.8 - 0.6 * math.exp(-0.3 * i)
        ypool, q, k, v = _mix_in(
            h, pos2, ln_mix[i][None], w_in[i].astype(BF16), pool_w[i].astype(BF16),
            pool_scale[i][None], jnp.tile(q_norm[i], n_groups)[None],
            jnp.tile(k_norm[i], n_groups)[None], invf, bd,
            batch=B, seq=S, tile=_pick_tile(S, 512))
        attn = functools.partial(_diff_attn, batch=B, seq=S, tq=_pick_tile(S, 512),
                                 lam_init=lam_init)
        score_bound = (HEAD_DIM * QK_SCALE * 1.02
                       * jnp.max(jnp.abs(q_norm[i])) * jnp.max(jnp.abs(k_norm[i])))
        o = lax.cond(score_bound <= EXP2_SAFE_RANGE,
                     functools.partial(attn, bounded=True), functools.partial(attn, bounded=False),
                     q, k, v, lambda_q1[i][None], lambda_k1[i][None], lambda_q2[i][None],
                     lambda_k2[i][None], subln[i][None])
        h1, xn, er, quota, ea = _peer_front(
            h, ypool, o, w_o[i].astype(BF16), ln_ffn[i][None], w_peer_q[i].astype(BF16),
            peer_subkeys[i].astype(BF16), tile=_pick_tile(T, 512))
        vt_blocks = (peer_v[i].astype(BF16).reshape(n_exp // PEER_EXPERT_BLOCK, PEER_EXPERT_BLOCK, D)
                     .transpose(0, 2, 1))
        h2 = _peer_dense(xn, h1, peer_u[i].astype(BF16), vt_blocks, er, quota, ea,
                         tile=_pick_tile(T, 512))
        h = _ple(h2, p[i].reshape(T, -1), ln_pe[i][None], w_pe_gate[i].astype(BF16),
                 w_pe_proj[i].astype(BF16), tile=_pick_tile(T, 512))
    return h.reshape(B, S, D)
```
